```python
import jax
import jax.numpy as jnp
from jax import lax
import numpy as np

D_MODEL = 2048
BATCH = 8
SEQ = 2048
DEPTH = 1
DEC_BATCH = 128
DEC_SEQ = 1
PAST_LEN = 16384
PAGE_SIZE = 128

N_META = 16
Q_BLOCK = 128
RMS_EPS = 1e-6
NEG_INF = -1e30
FOX_HEADS = 8
FOX_KV_HEADS = 2
FOX_GROUP = FOX_HEADS // FOX_KV_HEADS
FOX_HEAD_DIM = 128
FOX_Q_W = FOX_HEADS * FOX_HEAD_DIM
FOX_KV_W = FOX_KV_HEADS * FOX_HEAD_DIM
FOX_SCALE = FOX_HEAD_DIM ** -0.5
FORGET_BIAS_INIT = 3.0
MLA_HEADS = 8
MLA_Q_LORA = 512
MLA_KV_LORA = 512
MLA_NOPE = 128
MLA_ROPE = 64
MLA_V = 128
MLA_SCALE = (MLA_NOPE + MLA_ROPE) ** -0.5
ROPE_BASE = 10000.0
D_FF = ((8 * D_MODEL + 3 * 256 - 1) // (3 * 256)) * 256
IN_SIZES = (FOX_Q_W, FOX_KV_W, FOX_KV_W, FOX_HEADS, MLA_Q_LORA, MLA_KV_LORA, MLA_ROPE, D_MODEL, D_MODEL)
D_IN = sum(IN_SIZES)

kernel_name = 'hybrid_fox_mla_gated_decoder_step'


def _rmsnorm(x, g):
    xf = x.astype(jnp.float32)
    y = xf * lax.rsqrt(jnp.mean(xf * xf, axis=-1, keepdims=True) + RMS_EPS)
    return y.astype(x.dtype) * g


def _rope_angles(pos):
    inv_freq = ROPE_BASE ** (-jnp.arange(0, MLA_ROPE, 2, dtype=jnp.float32) / MLA_ROPE)
    return pos.astype(jnp.float32)[:, None] * inv_freq[None, :]


def _apply_rope(x, ang):
    half = MLA_ROPE // 2
    xf = x.astype(jnp.float32)
    x1, x2 = xf[..., :half], xf[..., half:]
    c, s = jnp.cos(ang), jnp.sin(ang)
    return jnp.concatenate([x1 * c - x2 * s, x1 * s + x2 * c], axis=-1).astype(x.dtype)


def _project_in(hn, ang, w_in, b_f, q_norm, kv_norm, w_uq, w_uk):
    b, s, _ = hn.shape
    offsets = [int(o) for o in np.cumsum(IN_SIZES)[:-1]]
    zq, zk, zv, zf, zcq, zckv, zkr, ga, gb = jnp.split(hn @ w_in, offsets, axis=-1)
    q = (_rmsnorm(zcq, q_norm) @ w_uq).reshape(b, s, MLA_HEADS, MLA_NOPE + MLA_ROPE)
    return {
        'fox_q': zq.reshape(b, s, FOX_HEADS, FOX_HEAD_DIM),
        'fox_k': zk.reshape(b, s, FOX_KV_HEADS, FOX_HEAD_DIM),
        'fox_v': zv.reshape(b, s, FOX_KV_HEADS, FOX_HEAD_DIM),
        'logf': jax.nn.log_sigmoid(zf.astype(jnp.float32) + b_f.astype(jnp.float32)),
        'q_lat': jnp.einsum('bshd,chd->bshc', q[..., :MLA_NOPE], w_uk),
        'q_rope': _apply_rope(q[..., MLA_NOPE:], ang[:, None, :]),
        'ckv': _rmsnorm(zckv, kv_norm),
        'k_rope': _apply_rope(zkr, ang),
        'gate_fox': ga,
        'gate_mla': gb,
    }


def _to_blocks(a, bq):
    b, n = a.shape[:2]
    return a.reshape((b, n // bq, bq) + a.shape[2:]).swapaxes(0, 1)


def _from_blocks(o):
    nb, b, bq = o.shape[:3]
    return o.swapaxes(0, 1).reshape((b, nb * bq) + o.shape[3:])


def _fox_prompt(q, k, v, log_cum, lo, hi):
    b = q.shape[0]
    bq = min(Q_BLOCK, hi - lo)
    k_keys, v_keys = k[:, :hi], v[:, :hi]
    c_keys = log_cum[:, :hi].transpose(0, 2, 1).reshape(b, FOX_KV_HEADS, FOX_GROUP, 1, hi)
    key_pos = jnp.arange(hi)

    def block(args):
        qb, cb, pos = args
        qg = qb.reshape(b, bq, FOX_KV_HEADS, FOX_GROUP, FOX_HEAD_DIM)
        s = jnp.einsum('bqgrd,bkgd->bgrqk', qg, k_keys).astype(jnp.float32) * FOX_SCALE
        c_q = cb.transpose(0, 2, 1).reshape(b, FOX_KV_HEADS, FOX_GROUP, bq, 1)
        s = jnp.where(key_pos[None, :] <= pos[:, None], s + c_q - c_keys, NEG_INF)
        p = jax.nn.softmax(s, axis=-1).astype(v_keys.dtype)
        o = jnp.einsum('bgrqk,bkgd->bqgrd', p, v_keys)
        return o.reshape(b, bq, FOX_Q_W)

    xs = (_to_blocks(q[:, lo:hi], bq), _to_blocks(log_cum[:, lo:hi], bq), jnp.arange(lo, hi).reshape(-1, bq))
    return _from_blocks(lax.map(block, xs))


def _mla_prompt(q_lat, q_rope, ckv, k_rope, w_uv, lo, hi):
    b = q_lat.shape[0]
    bq = min(Q_BLOCK, hi - lo)
    c_keys, r_keys = ckv[:, :hi], k_rope[:, :hi]
    key_pos = jnp.arange(hi)

    def block(args):
        qlb, qrb, pos = args
        s = (jnp.einsum('bqhc,bkc->bhqk', qlb, c_keys)
             + jnp.einsum('bqhr,bkr->bhqk', qrb, r_keys)).astype(jnp.float32) * MLA_SCALE
        s = jnp.where(key_pos[None, :] <= pos[:, None], s, NEG_INF)
        p = jax.nn.softmax(s, axis=-1).astype(c_keys.dtype)
        o_lat = jnp.einsum('bhqk,bkc->bqhc', p, c_keys)
        return jnp.einsum('bqhc,chd->bqhd', o_lat, w_uv).reshape(b, bq, MLA_HEADS * MLA_V)

    xs = (_to_blocks(q_lat[:, lo:hi], bq), _to_blocks(q_rope[:, lo:hi], bq), jnp.arange(lo, hi).reshape(-1, bq))
    return _from_blocks(lax.map(block, xs))


def _online_update(carry, s, pv):
    m, l, acc = carry
    m_new = jnp.maximum(m, jnp.max(s, axis=-1))
    corr = jnp.exp(m - m_new)
    p = jnp.exp(s - m_new[..., None])
    return (m_new, l * corr + jnp.sum(p, axis=-1), acc * corr[..., None] + pv(p))


def _fox_sample(q, k_new, v_new, logf_new, cache_k, cache_v, cache_logf, layer, page_table):
    db, s_new = q.shape[:2]
    n_pages = page_table.shape[1]
    logf_past = cache_logf[layer, page_table].astype(jnp.float32).reshape(db, n_pages * PAGE_SIZE, FOX_HEADS)
    suffix = lax.cumsum(logf_past, axis=1, reverse=True) - logf_past
    suffix = suffix.reshape(db, n_pages, PAGE_SIZE, FOX_KV_HEADS, FOX_GROUP).transpose(1, 0, 3, 4, 2)
    cn = lax.cumsum(logf_new, axis=1).transpose(0, 2, 1).reshape(db, FOX_KV_HEADS, FOX_GROUP, s_new, 1)
    qg = q.reshape(db, s_new, FOX_KV_HEADS, FOX_GROUP, FOX_HEAD_DIM)
    stat_shape = (db, FOX_KV_HEADS, FOX_GROUP, s_new)
    init = (jnp.full(stat_shape, NEG_INF, jnp.float32), jnp.zeros(stat_shape, jnp.float32),
            jnp.zeros(stat_shape + (FOX_HEAD_DIM,), jnp.float32))

    def body(carry, xs):
        pid, suf = xs
        kp, vp = cache_k[layer, pid], cache_v[layer, pid]
        s = jnp.einsum('bqgrd,bkgd->bgrqk', qg, kp).astype(jnp.float32) * FOX_SCALE + cn + suf[:, :, :, None, :]
        return _online_update(carry, s, lambda p: jnp.einsum('bgrqk,bkgd->bgrqd', p, vp.astype(jnp.float32))), None

    carry, _ = lax.scan(body, init, (page_table.T, suffix))
    s = jnp.einsum('bqgrd,bkgd->bgrqk', qg, k_new).astype(jnp.float32) * FOX_SCALE + cn - cn.swapaxes(-1, -2)
    causal = jnp.arange(s_new)[None, :] <= jnp.arange(s_new)[:, None]
    s = jnp.where(causal, s, NEG_INF)
    m, l, acc = _online_update(carry, s, lambda p: jnp.einsum('bgrqk,bkgd->bgrqd', p, v_new.astype(jnp.float32)))
    o = acc / l[..., None]
    return o.transpose(0, 3, 1, 2, 4).reshape(db, s_new, FOX_Q_W).astype(q.dtype)


def _mla_sample(q_lat, q_rope, ckv_new, krope_new, cache_ckv, cache_krope, layer, page_table, w_uv):
    db, s_new = q_lat.shape[:2]
    stat_shape = (db, MLA_HEADS, s_new)
    init = (jnp.full(stat_shape, NEG_INF, jnp.float32), jnp.zeros(stat_shape, jnp.float32),
            jnp.zeros(stat_shape + (MLA_KV_LORA,), jnp.float32))

    def scores(cp, rp):
        return (jnp.einsum('bqhc,bkc->bhqk', q_lat, cp)
                + jnp.einsum('bqhr,bkr->bhqk', q_rope, rp)).astype(jnp.float32) * MLA_SCALE

    def body(carry, pid):
        cp, rp = cache_ckv[layer, pid], cache_krope[layer, pid]
        return _online_update(carry, scores(cp, rp), lambda p: jnp.einsum('bhqk,bkc->bhqc', p, cp.astype(jnp.float32))), None

    carry, _ = lax.scan(body, init, page_table.T)
    causal = jnp.arange(s_new)[None, :] <= jnp.arange(s_new)[:, None]
    s = jnp.where(causal, scores(ckv_new, krope_new), NEG_INF)
    m, l, acc = _online_update(carry, s, lambda p: jnp.einsum('bhqk,bkc->bhqc', p, ckv_new.astype(jnp.float32)))
    o_lat = (acc / l[..., None]).astype(ckv_new.dtype)
    return jnp.einsum('bhqc,chd->bqhd', o_lat, w_uv).reshape(db, s_new, MLA_HEADS * MLA_V)


def _merge_and_ffn(h, y_fox, y_mla, g_fox, g_mla, w_bf, w_bm, w_o, g_ffn, w_gate, w_up, w_down):
    mixed = jax.nn.sigmoid(g_fox) * (y_fox @ w_bf) + jax.nn.sigmoid(g_mla) * (y_mla @ w_bm)
    h = h + mixed @ w_o
    hn = _rmsnorm(h, g_ffn)
    return h + (jax.nn.silu(hn @ w_gate) * (hn @ w_up)) @ w_down


def setup_inputs(seed: int = 0) -> dict:
    key = jax.random.key(seed)
    ks = jax.random.split(key, 25)
    f32 = jnp.float32
    n_pages = PAST_LEN // PAGE_SIZE
    n_used = DEC_BATCH * n_pages
    n_pool = n_used + n_used // 4

    def nrm(k, shape, scale=1.0):
        return jax.random.normal(k, shape, f32) * scale

    def gain(k, shape):
        return 1.0 + 0.1 * jax.random.normal(k, shape, f32)

    page_table = jax.random.permutation(ks[7], n_pool)[:n_used].reshape(DEC_BATCH, n_pages).astype(jnp.int32)
    return {
        'x_prompt': nrm(ks[0], (BATCH, SEQ, D_MODEL)),
        'x_sample': nrm(ks[1], (DEC_BATCH, DEC_SEQ, D_MODEL)),
        'cache_fox_k': nrm(ks[2], (DEPTH, n_pool, PAGE_SIZE, FOX_KV_HEADS, FOX_HEAD_DIM)),
        'cache_fox_v': nrm(ks[3], (DEPTH, n_pool, PAGE_SIZE, FOX_KV_HEADS, FOX_HEAD_DIM)),
        'cache_fox_logf': jax.nn.log_sigmoid(FORGET_BIAS_INIT + nrm(ks[4], (DEPTH, n_pool, PAGE_SIZE, FOX_HEADS))),
        'cache_mla_ckv': nrm(ks[5], (DEPTH, n_pool, PAGE_SIZE, MLA_KV_LORA)),
        'cache_mla_krope': nrm(ks[6], (DEPTH, n_pool, PAGE_SIZE, MLA_ROPE)),
        'page_table': page_table,
        'meta_tokens': nrm(ks[8], (N_META, D_MODEL)),
        'attn_norm': gain(ks[9], (DEPTH, D_MODEL)),
        'w_in': nrm(ks[10], (DEPTH, D_MODEL, D_IN), D_MODEL ** -0.5),
        'fox_forget_bias': FORGET_BIAS_INIT + 0.5 * nrm(ks[11], (DEPTH, FOX_HEADS)),
        'mla_q_norm': gain(ks[12], (DEPTH, MLA_Q_LORA)),
        'mla_kv_norm': gain(ks[13], (DEPTH, MLA_KV_LORA)),
        'mla_w_uq': nrm(ks[14], (DEPTH, MLA_Q_LORA, MLA_HEADS * (MLA_NOPE + MLA_ROPE)), MLA_Q_LORA ** -0.5),
        'mla_w_uk': nrm(ks[15], (DEPTH, MLA_KV_LORA, MLA_HEADS, MLA_NOPE), MLA_KV_LORA ** -0.5),
        'mla_w_uv': nrm(ks[16], (DEPTH, MLA_KV_LORA, MLA_HEADS, MLA_V), MLA_KV_LORA ** -0.5),
        'w_branch_fox': nrm(ks[17], (DEPTH, FOX_Q_W, D_MODEL), FOX_Q_W ** -0.5),
        'w_branch_mla': nrm(ks[18], (DEPTH, MLA_HEADS * MLA_V, D_MODEL), (MLA_HEADS * MLA_V) ** -0.5),
        'w_out': nrm(ks[19], (DEPTH, D_MODEL, D_MODEL), D_MODEL ** -0.5),
        'ffn_norm': gain(ks[20], (DEPTH, D_MODEL)),
        'w_gate': nrm(ks[21], (DEPTH, D_MODEL, D_FF), D_MODEL ** -0.5),
        'w_up': nrm(ks[22], (DEPTH, D_MODEL, D_FF), D_MODEL ** -0.5),
        'w_down': nrm(ks[23], (DEPTH, D_FF, D_MODEL), D_FF ** -0.5),
        'final_norm': gain(ks[24], (D_MODEL,)),
    }


def reference(x_prompt, x_sample, cache_fox_k, cache_fox_v, cache_fox_logf, cache_mla_ckv, cache_mla_krope,
              page_table, meta_tokens, attn_norm, w_in, fox_forget_bias, mla_q_norm, mla_kv_norm, mla_w_uq,
              mla_w_uk, mla_w_uv, w_branch_fox, w_branch_mla, w_out, ffn_norm, w_gate, w_up, w_down, final_norm):
    b, seq, _ = x_prompt.shape
    length = seq + N_META
    meta = jnp.broadcast_to(meta_tokens[None].astype(x_prompt.dtype), (b, N_META, D_MODEL))
    h_p = jnp.concatenate([meta, x_prompt], axis=1)
    h_s = x_sample
    past_len = page_table.shape[1] * PAGE_SIZE
    ang_p = _rope_angles(jnp.arange(length))
    ang_s = _rope_angles(past_len + jnp.arange(x_sample.shape[1]))

    pk, pv, pf, pc, pr = [], [], [], [], []
    sk, sv, sf, sc, sr = [], [], [], [], []
    for layer in range(DEPTH):
        proj_w = (w_in[layer], fox_forget_bias[layer], mla_q_norm[layer], mla_kv_norm[layer],
                  mla_w_uq[layer], mla_w_uk[layer])
        out_w = (w_branch_fox[layer], w_branch_mla[layer], w_out[layer], ffn_norm[layer],
                 w_gate[layer], w_up[layer], w_down[layer])

        p = _project_in(_rmsnorm(h_p, attn_norm[layer]), ang_p, *proj_w)
        log_cum = lax.cumsum(p['logf'], axis=1)
        spans = [(N_META, length)] if layer == DEPTH - 1 else [(0, N_META), (N_META, length)]
        q_lo = spans[0][0]
        y_fox = jnp.concatenate([_fox_prompt(p['fox_q'], p['fox_k'], p['fox_v'], log_cum, lo, hi)
                                 for lo, hi in spans], axis=1)
        y_mla = jnp.concatenate([_mla_prompt(p['q_lat'], p['q_rope'], p['ckv'], p['k_rope'], mla_w_uv[layer], lo, hi)
                                 for lo, hi in spans], axis=1)
        h_p = _merge_and_ffn(h_p[:, q_lo:], y_fox, y_mla, p['gate_fox'][:, q_lo:], p['gate_mla'][:, q_lo:], *out_w)
        pk.append(p['fox_k'].astype(cache_fox_k.dtype))
        pv.append(p['fox_v'].astype(cache_fox_v.dtype))
        pf.append(p['logf'].astype(cache_fox_logf.dtype))
        pc.append(p['ckv'].astype(cache_mla_ckv.dtype))
        pr.append(p['k_rope'].astype(cache_mla_krope.dtype))

        s = _project_in(_rmsnorm(h_s, attn_norm[layer]), ang_s, *proj_w)
        y_fox_s = _fox_sample(s['fox_q'], s['fox_k'], s['fox_v'], s['logf'], cache_fox_k, cache_fox_v,
                              cache_fox_logf, layer, page_table)
        y_mla_s = _mla_sample(s['q_lat'], s['q_rope'], s['ckv'], s['k_rope'], cache_mla_ckv, cache_mla_krope,
                              layer, page_table, mla_w_uv[layer])
        h_s = _merge_and_ffn(h_s, y_fox_s, y_mla_s, s['gate_fox'], s['gate_mla'], *out_w)
        sk.append(s['fox_k'].astype(cache_fox_k.dtype))
        sv.append(s['fox_v'].astype(cache_fox_v.dtype))
        sf.append(s['logf'].astype(cache_fox_logf.dtype))
        sc.append(s['ckv'].astype(cache_mla_ckv.dtype))
        sr.append(s['k_rope'].astype(cache_mla_krope.dtype))

    y_prompt = _rmsnorm(h_p, final_norm)
    y_sample = _rmsnorm(h_s, final_norm)
    return (y_prompt, y_sample,
            jnp.stack(pk), jnp.stack(pv), jnp.stack(pf), jnp.stack(pc), jnp.stack(pr),
            jnp.stack(sk), jnp.stack(sv), jnp.stack(sf), jnp.stack(sc), jnp.stack(sr))
```

```python
import functools

import numpy as np
import jax
import jax.numpy as jnp
from jax import lax
from jax.experimental import pallas as pl
from jax.experimental.pallas import tpu as pltpu

D_MODEL = 2048
N_META = 16
RMS_EPS = 1e-6
NEG_INF = -1e30
FOX_HEADS = 8
FOX_KV_HEADS = 2
FOX_GROUP = FOX_HEADS // FOX_KV_HEADS
FOX_HEAD_DIM = 128
FOX_Q_W = FOX_HEADS * FOX_HEAD_DIM
FOX_KV_W = FOX_KV_HEADS * FOX_HEAD_DIM
FOX_SCALE = FOX_HEAD_DIM ** -0.5
MLA_HEADS = 8
MLA_Q_LORA = 512
MLA_KV_LORA = 512
MLA_NOPE = 128
MLA_ROPE = 64
MLA_V = 128
MLA_SCALE = (MLA_NOPE + MLA_ROPE) ** -0.5
ROPE_BASE = 10000.0
PAGE_SIZE = 128
LANE = 128
SMALL_ROWS = 256
AUG = 2 * LANE
DEC_PAGES = 8
VMEM_LIMIT = 56 * 1024 * 1024

bf16 = jnp.bfloat16
f32 = jnp.float32


def _cparams(sem):
    return pltpu.CompilerParams(dimension_semantics=sem, vmem_limit_bytes=VMEM_LIMIT)


def _rms(x, g):
    return x * lax.rsqrt(jnp.mean(x * x, axis=-1, keepdims=True) + RMS_EPS) * g


def _dot(a, b):
    return jnp.dot(a, b, preferred_element_type=f32)


def _dot_nt(a, b):
    return lax.dot_general(a, b, (((1,), (1,)), ((), ())), preferred_element_type=f32)


def _split3(c):
    hi = c.astype(bf16)
    r1 = c - hi.astype(f32)
    mid = r1.astype(bf16)
    lo = (r1 - mid.astype(f32)).astype(bf16)
    return hi, mid, lo


def _qkv_kernel(x_ref, g_ref, w_ref, q_ref, k_ref, v_ref, vb_ref):
    hn = _rms(x_ref[...], g_ref[...]).astype(bf16)
    z = _dot(hn, w_ref[...])
    q_ref[...] = (z[:, :FOX_Q_W] * FOX_SCALE).astype(bf16)
    k_ref[...] = z[:, FOX_Q_W:FOX_Q_W + FOX_KV_W]
    v = z[:, FOX_Q_W + FOX_KV_W:]
    v_ref[...] = v
    vb_ref[...] = v.astype(bf16)


def _qkv_proj(x, g, w, tm):
    r = x.shape[0]
    n = w.shape[1]
    row = lambda i: (i, 0)
    fix = lambda i: (0, 0)
    return pl.pallas_call(
        _qkv_kernel,
        grid=(r // tm,),
        in_specs=[pl.BlockSpec((tm, D_MODEL), row), pl.BlockSpec((1, D_MODEL), fix),
                  pl.BlockSpec((D_MODEL, n), fix)],
        out_specs=[pl.BlockSpec((tm, FOX_Q_W), row), pl.BlockSpec((tm, FOX_KV_W), row),
                   pl.BlockSpec((tm, FOX_KV_W), row), pl.BlockSpec((tm, FOX_KV_W), row)],
        out_shape=[jax.ShapeDtypeStruct((r, FOX_Q_W), bf16), jax.ShapeDtypeStruct((r, FOX_KV_W), f32),
                   jax.ShapeDtypeStruct((r, FOX_KV_W), f32), jax.ShapeDtypeStruct((r, FOX_KV_W), bf16)],
        compiler_params=_cparams(("arbitrary",)),
        name="qkv_proj",
    )(x, g, w)


def _lat_kernel(x_ref, g_ref, w_ref, qn_ref, kvn_ref, bf_ref, cos_ref, sin_ref,
                cq_ref, ckv_ref, ckvb_ref, kr_ref, krb_ref, lf_ref, lfp_ref):
    hn = _rms(x_ref[...], g_ref[...]).astype(bf16)
    z = _dot(hn, w_ref[...])
    o = 0
    cq_ref[...] = _rms(z[:, o:o + MLA_Q_LORA], qn_ref[...]).astype(bf16)
    o += MLA_Q_LORA
    ckv = _rms(z[:, o:o + MLA_KV_LORA], kvn_ref[...])
    ckv_ref[...] = ckv
    ckvb_ref[...] = ckv.astype(bf16)
    o += MLA_KV_LORA
    kr = z[:, o:o + LANE] * cos_ref[...] + z[:, o + LANE:o + 2 * LANE] * sin_ref[...]
    kr_ref[...] = kr[:, :MLA_ROPE]
    krb_ref[...] = kr.astype(bf16)
    o += 2 * LANE
    zf = z[:, o:o + LANE] + bf_ref[...]
    lf = jnp.minimum(zf, 0.0) - jnp.log(1.0 + jnp.exp(-jnp.abs(zf)))
    lane = lax.broadcasted_iota(jnp.int32, lf.shape, 1)
    lf = jnp.where(lane < FOX_HEADS, lf, 0.0)
    lf_ref[...] = lf[:, :FOX_HEADS]
    lfp_ref[...] = lf


def _lat_proj(x, g, w, qn, kvn, bfp, cos, sin, tm):
    r = x.shape[0]
    n = w.shape[1]
    ntab = cos.shape[0] // tm
    row = lambda i: (i, 0)
    fix = lambda i: (0, 0)
    tab = lambda i: (i % ntab, 0)
    return pl.pallas_call(
        _lat_kernel,
        grid=(r // tm,),
        in_specs=[pl.BlockSpec((tm, D_MODEL), row), pl.BlockSpec((1, D_MODEL), fix),
                  pl.BlockSpec((D_MODEL, n), fix), pl.BlockSpec((1, MLA_Q_LORA), fix),
                  pl.BlockSpec((1, MLA_KV_LORA), fix), pl.BlockSpec((1, LANE), fix),
                  pl.BlockSpec((tm, LANE), tab), pl.BlockSpec((tm, LANE), tab)],
        out_specs=[pl.BlockSpec((tm, MLA_Q_LORA), row), pl.BlockSpec((tm, MLA_KV_LORA), row),
                   pl.BlockSpec((tm, MLA_KV_LORA), row), pl.BlockSpec((tm, MLA_ROPE), row),
                   pl.BlockSpec((tm, LANE), row), pl.BlockSpec((tm, FOX_HEADS), row),
                   pl.BlockSpec((tm, LANE), row)],
        out_shape=[jax.ShapeDtypeStruct((r, MLA_Q_LORA), bf16), jax.ShapeDtypeStruct((r, MLA_KV_LORA), f32),
                   jax.ShapeDtypeStruct((r, MLA_KV_LORA), bf16), jax.ShapeDtypeStruct((r, MLA_ROPE), f32),
                   jax.ShapeDtypeStruct((r, LANE), bf16), jax.ShapeDtypeStruct((r, FOX_HEADS), f32),
                   jax.ShapeDtypeStruct((r, LANE), f32)],
        compiler_params=_cparams(("arbitrary",)),
        name="lat_proj",
    )(x, g, w, qn, kvn, bfp, cos, sin)


def _gate_kernel(x_ref, g_ref, w_ref, o_ref, hn_ref):
    @pl.when(pl.program_id(1) == 0)
    def _():
        hn_ref[...] = _rms(x_ref[...], g_ref[...]).astype(bf16)

    z = _dot(hn_ref[...], w_ref[...])
    o_ref[...] = (1.0 / (1.0 + jnp.exp(-z))).astype(bf16)


def _gate_proj(x, g, w, tm, tn):
    r = x.shape[0]
    n = w.shape[1]
    return pl.pallas_call(
        _gate_kernel,
        grid=(r // tm, n // tn),
        in_specs=[pl.BlockSpec((tm, D_MODEL), lambda i, j: (i, 0)),
                  pl.BlockSpec((1, D_MODEL), lambda i, j: (0, 0)),
                  pl.BlockSpec((D_MODEL, tn), lambda i, j: (0, j))],
        out_specs=pl.BlockSpec((tm, tn), lambda i, j: (i, j)),
        out_shape=jax.ShapeDtypeStruct((r, n), bf16),
        scratch_shapes=[pltpu.VMEM((tm, D_MODEL), bf16)],
        compiler_params=_cparams(("arbitrary", "arbitrary")),
        name="gate_proj",
    )(x, g, w)


def _mla_q_kernel(cq_ref, w_ref, cos_ref, sin_ref, q_ref):
    z = _dot(cq_ref[...], w_ref[...])
    cos = cos_ref[...]
    sin = sin_ref[...]
    for h in range(MLA_HEADS):
        o = 3 * LANE * h
        q_ref[:, AUG * h:AUG * h + LANE] = (z[:, o:o + LANE] * MLA_SCALE).astype(q_ref.dtype)
        rope = z[:, o + LANE:o + 2 * LANE] * cos + z[:, o + 2 * LANE:o + 3 * LANE] * sin
        q_ref[:, AUG * h + LANE:AUG * (h + 1)] = (rope * MLA_SCALE).astype(q_ref.dtype)


def _mla_q(cq, w, cos, sin, tm, out_dtype):
    r = cq.shape[0]
    ntab = cos.shape[0] // tm
    row = lambda i: (i, 0)
    fix = lambda i: (0, 0)
    tab = lambda i: (i % ntab, 0)
    return pl.pallas_call(
        _mla_q_kernel,
        grid=(r // tm,),
        in_specs=[pl.BlockSpec((tm, MLA_Q_LORA), row), pl.BlockSpec(w.shape, fix),
                  pl.BlockSpec((tm, LANE), tab), pl.BlockSpec((tm, LANE), tab)],
        out_specs=pl.BlockSpec((tm, MLA_HEADS * AUG), row),
        out_shape=jax.ShapeDtypeStruct((r, MLA_HEADS * AUG), out_dtype),
        compiler_params=_cparams(("arbitrary",)),
        name="mla_q",
    )(cq, w, cos, sin)


def _mla_kv_kernel(ckv_ref, kr_ref, wk_ref, wv_ref, k_ref, v_ref):
    ckv = ckv_ref[...]
    kn = _dot(ckv, wk_ref[...])
    kr = kr_ref[...]
    for h in range(MLA_HEADS):
        k_ref[:, AUG * h:AUG * h + LANE] = kn[:, MLA_NOPE * h:MLA_NOPE * (h + 1)].astype(bf16)
        k_ref[:, AUG * h + LANE:AUG * (h + 1)] = kr
    v_ref[...] = _dot(ckv, wv_ref[...]).astype(bf16)


def _mla_kv(ckvb, krb, wk, wv, tm):
    r = ckvb.shape[0]
    row = lambda i: (i, 0)
    fix = lambda i: (0, 0)
    return pl.pallas_call(
        _mla_kv_kernel,
        grid=(r // tm,),
        in_specs=[pl.BlockSpec((tm, MLA_KV_LORA), row), pl.BlockSpec((tm, LANE), row),
                  pl.BlockSpec(wk.shape, fix), pl.BlockSpec(wv.shape, fix)],
        out_specs=[pl.BlockSpec((tm, MLA_HEADS * AUG), row), pl.BlockSpec((tm, MLA_HEADS * MLA_V), row)],
        out_shape=[jax.ShapeDtypeStruct((r, MLA_HEADS * AUG), bf16),
                   jax.ShapeDtypeStruct((r, MLA_HEADS * MLA_V), bf16)],
        compiler_params=_cparams(("arbitrary",)),
        name="mla_kv",
    )(ckvb, krb, wk, wv)


def _qlat_kernel(q_ref, w_ref, o_ref):
    o_ref[...] = _dot(q_ref[...], w_ref[0])


def _mla_qlat(qfull, wukt):
    r = qfull.shape[0]
    return pl.pallas_call(
        _qlat_kernel,
        grid=(MLA_HEADS,),
        in_specs=[pl.BlockSpec((r, LANE), lambda h: (0, 2 * h)),
                  pl.BlockSpec((1, MLA_NOPE, MLA_KV_LORA), lambda h: (h, 0, 0))],
        out_specs=pl.BlockSpec((r, MLA_KV_LORA), lambda h: (0, h)),
        out_shape=jax.ShapeDtypeStruct((r, MLA_HEADS * MLA_KV_LORA), f32),
        compiler_params=_cparams(("arbitrary",)),
        name="mla_qlat",
    )(qfull, wukt)


def _uv_kernel(o_ref, w_ref, y_ref):
    y_ref[...] = _dot(o_ref[0], w_ref[0]).astype(y_ref.dtype)


def _mla_uv(olat_t, wuvh):
    r = olat_t.shape[1]
    return pl.pallas_call(
        _uv_kernel,
        grid=(MLA_HEADS,),
        in_specs=[pl.BlockSpec((1, r, MLA_KV_LORA), lambda h: (h, 0, 0)),
                  pl.BlockSpec((1, MLA_KV_LORA, MLA_V), lambda h: (h, 0, 0))],
        out_specs=pl.BlockSpec((r, MLA_V), lambda h: (0, h)),
        out_shape=jax.ShapeDtypeStruct((r, MLA_HEADS * MLA_V), bf16),
        compiler_params=_cparams(("arbitrary",)),
        name="mla_uv",
    )(olat_t, wuvh)


def _cumsum_kernel(l_ref, init_ref, c_ref, carry_ref):
    @pl.when(pl.program_id(1) == 0)
    def _():
        carry_ref[...] = init_ref[...]

    blk = l_ref[0]
    r = lax.broadcasted_iota(jnp.int32, (LANE, LANE), 0)
    c = lax.broadcasted_iota(jnp.int32, (LANE, LANE), 1)
    tri = jnp.where(c <= r, 1.0, 0.0).astype(bf16)
    hi, mid, lo = _split3(blk)
    within = _dot(tri, hi) + _dot(tri, mid) + _dot(tri, lo)
    out = within + carry_ref[...]
    c_ref[0] = out
    carry_ref[...] = out[LANE - 1:LANE, :]


def _cumsum(lf, init):
    nb, t, _ = lf.shape
    return pl.pallas_call(
        _cumsum_kernel,
        grid=(nb, t // LANE),
        in_specs=[pl.BlockSpec((1, LANE, LANE), lambda b, j: (b, j, 0)),
                  pl.BlockSpec((1, LANE), lambda b, j: (0, 0))],
        out_specs=pl.BlockSpec((1, LANE, LANE), lambda b, j: (b, j, 0)),
        out_shape=jax.ShapeDtypeStruct((nb, t, LANE), f32),
        scratch_shapes=[pltpu.VMEM((1, LANE), f32)],
        compiler_params=_cparams(("arbitrary", "arbitrary")),
        name="logf_cumsum",
    )(lf, init)


def _fox_pack_kernel(q_ref, k_ref, c_ref, pq_ref, pk_ref, oq_ref, ok_ref, qa_ref, ka_ref):
    hi, mid, lo = _split3(c_ref[...])
    eq = _dot(hi, pq_ref[0]) + _dot(mid, pq_ref[1]) + _dot(lo, pq_ref[2]) + oq_ref[...]
    ek = _dot(hi, pk_ref[0]) + _dot(mid, pk_ref[1]) + _dot(lo, pk_ref[2]) + ok_ref[...]
    for h in range(FOX_HEADS):
        qa_ref[:, AUG * h:AUG * h + LANE] = q_ref[:, LANE * h:LANE * (h + 1)]
        qa_ref[:, AUG * h + LANE:AUG * (h + 1)] = eq[:, LANE * h:LANE * (h + 1)].astype(bf16)
    for g in range(FOX_KV_HEADS):
        ka_ref[:, AUG * g:AUG * g + LANE] = k_ref[:, LANE * g:LANE * (g + 1)].astype(bf16)
        ka_ref[:, AUG * g + LANE:AUG * (g + 1)] = ek[:, LANE * g:LANE * (g + 1)].astype(bf16)


def _fox_pack(q, k, c, tm):
    r = q.shape[0]
    pq = np.zeros((3, LANE, FOX_HEADS * LANE), np.float32)
    oq = np.zeros((1, FOX_HEADS * LANE), np.float32)
    pk = np.zeros((3, LANE, FOX_KV_HEADS * LANE), np.float32)
    ok = np.zeros((1, FOX_KV_HEADS * LANE), np.float32)
    for h in range(FOX_HEADS):
        g, rr = divmod(h, FOX_GROUP)
        for x in range(3):
            pq[x, h, h * LANE + x] = 1.0
            oq[0, h * LANE + 3 + 3 * rr + x] = 1.0
            pk[x, h, g * LANE + 3 + 3 * rr + x] = -1.0
    for g in range(FOX_KV_HEADS):
        ok[0, g * LANE:g * LANE + 3] = 1.0
    row = lambda i: (i, 0)
    fix2 = lambda i: (0, 0)
    fix3 = lambda i: (0, 0, 0)
    return pl.pallas_call(
        _fox_pack_kernel,
        grid=(r // tm,),
        in_specs=[pl.BlockSpec((tm, FOX_Q_W), row), pl.BlockSpec((tm, FOX_KV_W), row),
                  pl.BlockSpec((tm, LANE), row), pl.BlockSpec(pq.shape, fix3), pl.BlockSpec(pk.shape, fix3),
                  pl.BlockSpec(oq.shape, fix2), pl.BlockSpec(ok.shape, fix2)],
        out_specs=[pl.BlockSpec((tm, FOX_HEADS * AUG), row), pl.BlockSpec((tm, FOX_KV_HEADS * AUG), row)],
        out_shape=[jax.ShapeDtypeStruct((r, FOX_HEADS * AUG), bf16),
                   jax.ShapeDtypeStruct((r, FOX_KV_HEADS * AUG), bf16)],
        compiler_params=_cparams(("arbitrary",)),
        name="fox_pack",
    )(q, k, c, jnp.asarray(pq, bf16), jnp.asarray(pk, bf16), jnp.asarray(oq), jnp.asarray(ok))


def _flash_kernel(q_ref, k_ref, v_ref, km_ref, vm_ref, o_ref, *, tq):
    qi = pl.program_id(2)
    q = q_ref[...]

    def step(k, v, carry, mask):
        m, l, acc = carry
        s = _dot_nt(q, k)
        if mask is not None:
            s = jnp.where(mask, s, NEG_INF)
        m_new = jnp.maximum(m, jnp.max(s, axis=-1, keepdims=True))
        alpha = jnp.exp(m - m_new)
        p = jnp.exp(s - m_new)
        l = alpha * l + jnp.sum(p, axis=-1, keepdims=True)
        acc = alpha * acc + _dot(p.astype(v.dtype), v)
        return m_new, l, acc

    carry = (jnp.full((tq, 1), NEG_INF, f32), jnp.zeros((tq, 1), f32), jnp.zeros((tq, LANE), f32))
    meta_col = lax.broadcasted_iota(jnp.int32, (tq, LANE), 1)
    carry = step(km_ref[...], vm_ref[...], carry, meta_col < N_META)

    def body(j, carry):
        off = pl.multiple_of(j * tq, tq)
        return step(k_ref[pl.ds(off, tq), :], v_ref[pl.ds(off, tq), :], carry, None)

    carry = lax.fori_loop(0, qi, body, carry)
    off = pl.multiple_of(qi * tq, tq)
    row = lax.broadcasted_iota(jnp.int32, (tq, tq), 0)
    col = lax.broadcasted_iota(jnp.int32, (tq, tq), 1)
    _, l, acc = step(k_ref[pl.ds(off, tq), :], v_ref[pl.ds(off, tq), :], carry, col <= row)
    o_ref[...] = (acc / l).astype(o_ref.dtype)


def _flash(q, k, v, km, vm, nb, seq, heads, group, tq):
    nq = seq // tq
    return pl.pallas_call(
        functools.partial(_flash_kernel, tq=tq),
        grid=(nb, heads, nq),
        in_specs=[pl.BlockSpec((tq, AUG), lambda b, h, i: (b * nq + i, h)),
                  pl.BlockSpec((seq, AUG), lambda b, h, i: (b, h // group)),
                  pl.BlockSpec((seq, LANE), lambda b, h, i: (b, h // group)),
                  pl.BlockSpec((LANE, AUG), lambda b, h, i: (0, h // group)),
                  pl.BlockSpec((LANE, LANE), lambda b, h, i: (0, h // group))],
        out_specs=pl.BlockSpec((tq, LANE), lambda b, h, i: (b * nq + i, h)),
        out_shape=jax.ShapeDtypeStruct((nb * seq, heads * LANE), bf16),
        compiler_params=_cparams(("arbitrary", "arbitrary", "arbitrary")),
        name="prompt_attn",
    )(q, k, v, km, vm)


def _decode_kernel(pt_ref, qf_ref, cn_ref, kn_ref, vn_ref, ql_ref, qr_ref, cnew_ref, rnew_ref,
                   ck_hbm, cv_hbm, clf_hbm, cc_hbm, cr_hbm,
                   of_ref, ol_ref,
                   kbuf, vbuf, lfbuf, cbuf, rbuf, sems, mf, lfs, accf, mm, lms, accm, carry,
                   *, n_chunks, n_pages):
    b = pl.program_id(0)
    c = pl.program_id(1)
    nb = pl.num_programs(0)
    t = b * n_chunks + c
    slot = t % 2
    npg = DEC_PAGES
    kv_rows = PAGE_SIZE * FOX_KV_HEADS

    def copies(bb, cc, sl):
        out = []
        for p in range(npg):
            pid = pt_ref[bb, (n_chunks - 1 - cc) * npg + p]
            out.append(pltpu.make_async_copy(ck_hbm.at[pl.ds(pid * kv_rows, kv_rows), :],
                                             kbuf.at[sl, pl.ds(p * kv_rows, kv_rows), :], sems.at[sl, 0]))
            out.append(pltpu.make_async_copy(cv_hbm.at[pl.ds(pid * kv_rows, kv_rows), :],
                                             vbuf.at[sl, pl.ds(p * kv_rows, kv_rows), :], sems.at[sl, 1]))
            out.append(pltpu.make_async_copy(clf_hbm.at[pid],
                                             lfbuf.at[sl, pl.ds(p * FOX_HEADS, FOX_HEADS), :], sems.at[sl, 2]))
            out.append(pltpu.make_async_copy(cc_hbm.at[pid],
                                             cbuf.at[sl, pl.ds(p * PAGE_SIZE, PAGE_SIZE), :], sems.at[sl, 3]))
            out.append(pltpu.make_async_copy(cr_hbm.at[pid],
                                             rbuf.at[sl, :, pl.ds(p * PAGE_SIZE, PAGE_SIZE)], sems.at[sl, 4]))
        return out

    @pl.when(t == 0)
    def _():
        for cp in copies(b, c, slot):
            cp.start()

    @pl.when(t + 1 < nb * n_chunks)
    def _():
        last = c + 1 == n_chunks
        for cp in copies(jnp.where(last, b + 1, b), jnp.where(last, 0, c + 1), 1 - slot):
            cp.start()

    @pl.when(c == 0)
    def _():
        mf[...] = jnp.full(mf.shape, NEG_INF, f32)
        lfs[...] = jnp.zeros(lfs.shape, f32)
        accf[...] = jnp.zeros(accf.shape, f32)
        mm[...] = jnp.full(mm.shape, NEG_INF, f32)
        lms[...] = jnp.zeros(lms.shape, f32)
        accm[...] = jnp.zeros(accm.shape, f32)
        carry[...] = jnp.zeros(carry.shape, f32)

    for cp in copies(b, c, slot):
        cp.wait()

    qf = qf_ref[0]
    nkv = npg * kv_rows
    stacked = lfbuf[slot]
    r = lax.broadcasted_iota(jnp.int32, (PAGE_SIZE, kv_rows), 0)
    cidx = lax.broadcasted_iota(jnp.int32, (PAGE_SIZE, kv_rows), 1)
    tri = jnp.where(r > cidx // FOX_KV_HEADS, 1.0, 0.0).astype(bf16)
    pieces = [stacked[p * FOX_HEADS:(p + 1) * FOX_HEADS, :] for p in range(npg)]
    hi, mid, lo = _split3(stacked)
    within = _dot(hi, tri) + _dot(mid, tri) + _dot(lo, tri)
    run = carry[:, :1]
    bias = [None] * npg
    for p in reversed(range(npg)):
        bias[p] = within[p * FOX_HEADS:(p + 1) * FOX_HEADS, :] + run
        run = run + jnp.sum(pieces[p], axis=-1, keepdims=True)
    carry[...] = jnp.broadcast_to(run, carry.shape)
    s = _dot_nt(qf, kbuf[slot]) + jnp.concatenate(bias, axis=1) + cn_ref[0][:, :1]
    head = lax.broadcasted_iota(jnp.int32, (FOX_HEADS, nkv), 0)
    kvrow = lax.broadcasted_iota(jnp.int32, (FOX_HEADS, nkv), 1)
    s = jnp.where(head // FOX_GROUP == kvrow % FOX_KV_HEADS, s, NEG_INF)
    m_old = mf[:, :1]
    m_new = jnp.maximum(m_old, jnp.max(s, axis=-1, keepdims=True))
    alpha = jnp.exp(m_old - m_new)
    p_f = jnp.exp(s - m_new)
    mf[...] = jnp.broadcast_to(m_new, mf.shape)
    lfs[...] = alpha * lfs[...] + jnp.sum(p_f, axis=-1, keepdims=True)
    accf[...] = alpha * accf[...] + _dot(p_f, vbuf[slot])

    cb = cbuf[slot]
    s = _dot_nt(ql_ref[0], cb) + _dot(qr_ref[0], rbuf[slot])
    m_old = mm[:, :1]
    m_new = jnp.maximum(m_old, jnp.max(s, axis=-1, keepdims=True))
    alpha = jnp.exp(m_old - m_new)
    p_m = jnp.exp(s - m_new)
    mm[...] = jnp.broadcast_to(m_new, mm.shape)
    lms[...] = alpha * lms[...] + jnp.sum(p_m, axis=-1, keepdims=True)
    accm[...] = alpha * accm[...] + _dot(p_m, cb)

    @pl.when(c == n_chunks - 1)
    def _():
        s_self = jnp.sum(qf * kn_ref[0], axis=-1, keepdims=True)
        m_old = mf[:, :1]
        m_new = jnp.maximum(m_old, s_self)
        alpha = jnp.exp(m_old - m_new)
        p_self = jnp.exp(s_self - m_new)
        l = alpha * lfs[:, :1] + p_self
        of_ref[0] = (alpha * accf[...] + p_self * vn_ref[0]) / l

        cnew = cnew_ref[0]
        s_self = (jnp.sum(ql_ref[0] * cnew, axis=-1, keepdims=True)
                  + jnp.sum(qr_ref[0] * rnew_ref[0], axis=-1, keepdims=True))
        m_old = mm[:, :1]
        m_new = jnp.maximum(m_old, s_self)
        alpha = jnp.exp(m_old - m_new)
        p_self = jnp.exp(s_self - m_new)
        l = alpha * lms[:, :1] + p_self
        ol_ref[0] = (alpha * accm[...] + p_self * cnew) / l


def _decode(page_table, qf, cn, kn, vn, ql, qr, cnew, rnew, ck, cv, clf, cc, cr):
    nb, n_pages = page_table.shape
    n_chunks = n_pages // DEC_PAGES
    kv_rows = PAGE_SIZE * FOX_KV_HEADS
    per_b = lambda shape: pl.BlockSpec((1,) + shape, lambda b, c, pt: (b, 0, 0))
    any_spec = pl.BlockSpec(memory_space=pl.ANY)
    grid_spec = pltpu.PrefetchScalarGridSpec(
        num_scalar_prefetch=1,
        grid=(nb, n_chunks),
        in_specs=[per_b((FOX_HEADS, FOX_HEAD_DIM)), per_b((FOX_HEADS, LANE)),
                  per_b((FOX_HEADS, FOX_HEAD_DIM)), per_b((FOX_HEADS, FOX_HEAD_DIM)),
                  per_b((MLA_HEADS, MLA_KV_LORA)), per_b((MLA_HEADS, MLA_ROPE)),
                  per_b((1, MLA_KV_LORA)), per_b((1, MLA_ROPE)),
                  any_spec, any_spec, any_spec, any_spec, any_spec],
        out_specs=[per_b((FOX_HEADS, FOX_HEAD_DIM)), per_b((MLA_HEADS, MLA_KV_LORA))],
        scratch_shapes=[
            pltpu.VMEM((2, DEC_PAGES * kv_rows, FOX_HEAD_DIM), f32),
            pltpu.VMEM((2, DEC_PAGES * kv_rows, FOX_HEAD_DIM), f32),
            pltpu.VMEM((2, DEC_PAGES * FOX_HEADS, PAGE_SIZE), f32),
            pltpu.VMEM((2, DEC_PAGES * PAGE_SIZE, MLA_KV_LORA), f32),
            pltpu.VMEM((2, MLA_ROPE, DEC_PAGES * PAGE_SIZE), f32),
            pltpu.SemaphoreType.DMA((2, 5)),
            pltpu.VMEM((FOX_HEADS, LANE), f32), pltpu.VMEM((FOX_HEADS, LANE), f32),
            pltpu.VMEM((FOX_HEADS, FOX_HEAD_DIM), f32),
            pltpu.VMEM((MLA_HEADS, LANE), f32), pltpu.VMEM((MLA_HEADS, LANE), f32),
            pltpu.VMEM((MLA_HEADS, MLA_KV_LORA), f32),
            pltpu.VMEM((FOX_HEADS, LANE), f32),
        ],
    )
    return pl.pallas_call(
        functools.partial(_decode_kernel, n_chunks=n_chunks, n_pages=n_pages),
        grid_spec=grid_spec,
        out_shape=[jax.ShapeDtypeStruct((nb, FOX_HEADS, FOX_HEAD_DIM), f32),
                   jax.ShapeDtypeStruct((nb, MLA_HEADS, MLA_KV_LORA), f32)],
        compiler_params=_cparams(("arbitrary", "arbitrary")),
        name="paged_decode_attn",
    )(page_table, qf, cn, kn, vn, ql, qr, cnew, rnew, ck, cv, clf, cc, cr)


def _merge_kernel(yf_ref, ym_ref, sf_ref, sm_ref, wf_ref, wm_ref, o_ref):
    a = sf_ref[...].astype(f32) * _dot(yf_ref[...], wf_ref[...])
    b = sm_ref[...].astype(f32) * _dot(ym_ref[...], wm_ref[...])
    o_ref[...] = (a + b).astype(bf16)


def _merge(yf, ym, sg, wf, wm, tm, tn):
    r = yf.shape[0]
    nj = D_MODEL // tn
    return pl.pallas_call(
        _merge_kernel,
        grid=(r // tm, nj),
        in_specs=[pl.BlockSpec((tm, FOX_Q_W), lambda i, j: (i, 0)),
                  pl.BlockSpec((tm, MLA_HEADS * MLA_V), lambda i, j: (i, 0)),
                  pl.BlockSpec((tm, tn), lambda i, j: (i, j)),
                  pl.BlockSpec((tm, tn), lambda i, j: (i, j + nj)),
                  pl.BlockSpec((FOX_Q_W, tn), lambda i, j: (0, j)),
                  pl.BlockSpec((MLA_HEADS * MLA_V, tn), lambda i, j: (0, j))],
        out_specs=pl.BlockSpec((tm, tn), lambda i, j: (i, j)),
        out_shape=jax.ShapeDtypeStruct((r, D_MODEL), bf16),
        compiler_params=_cparams(("arbitrary", "arbitrary")),
        name="branch_merge",
    )(yf, ym, sg, sg, wf, wm)


def _outproj_kernel(x_ref, m_ref, w_ref, o_ref):
    o_ref[...] = x_ref[...] + _dot(m_ref[...], w_ref[...])


def _outproj(x, mixed, w, tm, tn):
    r = x.shape[0]
    return pl.pallas_call(
        _outproj_kernel,
        grid=(r // tm, D_MODEL // tn),
        in_specs=[pl.BlockSpec((tm, tn), lambda i, j: (i, j)),
                  pl.BlockSpec((tm, D_MODEL), lambda i, j: (i, 0)),
                  pl.BlockSpec((D_MODEL, tn), lambda i, j: (0, j))],
        out_specs=pl.BlockSpec((tm, tn), lambda i, j: (i, j)),
        out_shape=jax.ShapeDtypeStruct((r, D_MODEL), f32),
        compiler_params=_cparams(("arbitrary", "arbitrary")),
        name="out_proj",
    )(x, mixed, w)


def _ffn_up_kernel(h_ref, g_ref, wg_ref, wu_ref, o_ref, hn_ref):
    @pl.when(pl.program_id(1) == 0)
    def _():
        hn_ref[...] = _rms(h_ref[...], g_ref[...]).astype(bf16)

    hn = hn_ref[...]
    a = _dot(hn, wg_ref[...])
    u = _dot(hn, wu_ref[...])
    o_ref[...] = (a / (1.0 + jnp.exp(-a)) * u).astype(bf16)


def _ffn_up(h, g, wg, wu, tm, tn):
    r = h.shape[0]
    n = wg.shape[1]
    return pl.pallas_call(
        _ffn_up_kernel,
        grid=(r // tm, n // tn),
        in_specs=[pl.BlockSpec((tm, D_MODEL), lambda i, j: (i, 0)),
                  pl.BlockSpec((1, D_MODEL), lambda i, j: (0, 0)),
                  pl.BlockSpec((D_MODEL, tn), lambda i, j: (0, j)),
                  pl.BlockSpec((D_MODEL, tn), lambda i, j: (0, j))],
        out_specs=pl.BlockSpec((tm, tn), lambda i, j: (i, j)),
        out_shape=jax.ShapeDtypeStruct((r, n), bf16),
        scratch_shapes=[pltpu.VMEM((tm, D_MODEL), bf16)],
        compiler_params=_cparams(("arbitrary", "arbitrary")),
        name="ffn_up",
    )(h, g, wg, wu)


def _ffn_down_kernel(h_ref, a_ref, w_ref, g_ref, y_ref, acc_ref):
    k = pl.program_id(1)

    @pl.when(k == 0)
    def _():
        acc_ref[...] = h_ref[...]

    acc_ref[...] += _dot(a_ref[...], w_ref[...])

    @pl.when(k == pl.num_programs(1) - 1)
    def _():
        y_ref[...] = _rms(acc_ref[...], g_ref[...])


def _ffn_down(h, act, w, g, tm, tk):
    r = h.shape[0]
    kdim = w.shape[0]
    return pl.pallas_call(
        _ffn_down_kernel,
        grid=(r // tm, kdim // tk),
        in_specs=[pl.BlockSpec((tm, D_MODEL), lambda i, k: (i, 0)),
                  pl.BlockSpec((tm, tk), lambda i, k: (i, k)),
                  pl.BlockSpec((tk, D_MODEL), lambda i, k: (k, 0)),
                  pl.BlockSpec((1, D_MODEL), lambda i, k: (0, 0))],
        out_specs=pl.BlockSpec((tm, D_MODEL), lambda i, k: (i, 0)),
        out_shape=jax.ShapeDtypeStruct((r, D_MODEL), f32),
        scratch_shapes=[pltpu.VMEM((tm, D_MODEL), f32)],
        compiler_params=_cparams(("arbitrary", "arbitrary")),
        name="ffn_down",
    )(h, act, w, g)


def _rope_tables(pos):
    inv_freq = ROPE_BASE ** (-jnp.arange(0, MLA_ROPE, 2, dtype=f32) / MLA_ROPE)
    ang = pos.astype(f32)[:, None] * inv_freq[None, :]
    c, s = jnp.cos(ang), jnp.sin(ang)
    z = jnp.zeros((pos.shape[0], LANE - MLA_ROPE), f32)
    return jnp.concatenate([c, c, z], axis=1), jnp.concatenate([-s, s, z], axis=1)


def _swap_halves(w):
    half = w.shape[-1] // 2
    return jnp.concatenate([w[..., half:], w[..., :half]], axis=-1)


def kernel(x_prompt, x_sample, cache_fox_k, cache_fox_v, cache_fox_logf, cache_mla_ckv, cache_mla_krope,
           page_table, meta_tokens, attn_norm, w_in, fox_forget_bias, mla_q_norm, mla_kv_norm, mla_w_uq,
           mla_w_uk, mla_w_uv, w_branch_fox, w_branch_mla, w_out, ffn_norm, w_gate, w_up, w_down, final_norm):
    nb, seq, _ = x_prompt.shape
    db = x_sample.shape[0]
    n_pool = cache_fox_k.shape[1]
    past_len = page_table.shape[1] * PAGE_SIZE
    layer = 0

    w0 = w_in[layer]
    o_f = FOX_Q_W + 2 * FOX_KV_W
    o_cq = o_f + FOX_HEADS
    o_ckv = o_cq + MLA_Q_LORA
    o_kr = o_ckv + MLA_KV_LORA
    o_g = o_kr + MLA_ROPE
    w_qkv = w0[:, :o_f].astype(bf16)
    w_kr = w0[:, o_kr:o_g]
    zpad = lambda n: jnp.zeros((D_MODEL, n), f32)
    w_lat = jnp.concatenate([w0[:, o_cq:o_kr], w_kr, zpad(LANE - MLA_ROPE), _swap_halves(w_kr),
                             zpad(LANE - MLA_ROPE), w0[:, o_f:o_cq], zpad(LANE - FOX_HEADS)], axis=1).astype(bf16)
    w_g = w0[:, o_g:].astype(bf16)
    bias_f = jnp.concatenate([fox_forget_bias[layer], jnp.zeros((LANE - FOX_HEADS,), f32)])[None]
    wuq = mla_w_uq[layer].reshape(MLA_Q_LORA, MLA_HEADS, MLA_NOPE + MLA_ROPE)
    zq = jnp.zeros((MLA_Q_LORA, MLA_HEADS, LANE - MLA_ROPE), f32)
    wuq = jnp.concatenate([wuq[..., :MLA_NOPE], wuq[..., MLA_NOPE:], zq,
                           _swap_halves(wuq[..., MLA_NOPE:]), zq], axis=-1)
    wuq = wuq.reshape(MLA_Q_LORA, MLA_HEADS * 3 * LANE).astype(bf16)
    wuk = mla_w_uk[layer].reshape(MLA_KV_LORA, MLA_HEADS * MLA_NOPE).astype(bf16)
    wuv = mla_w_uv[layer].reshape(MLA_KV_LORA, MLA_HEADS * MLA_V).astype(bf16)
    wuk_t = mla_w_uk[layer].transpose(1, 2, 0)
    wuv_h = mla_w_uv[layer].transpose(1, 0, 2).astype(bf16)
    w_bf = w_branch_fox[layer].astype(bf16)
    w_bm = w_branch_mla[layer].astype(bf16)
    w_o = w_out[layer].astype(bf16)
    w_ga = w_gate[layer].astype(bf16)
    w_u = w_up[layer].astype(bf16)
    w_d = w_down[layer].astype(bf16)
    g_attn = attn_norm[layer][None]
    g_ffn = ffn_norm[layer][None]
    g_fin = final_norm[None]
    qn = mla_q_norm[layer][None]
    kvn = mla_kv_norm[layer][None]

    x_main = x_prompt.reshape(nb * seq, D_MODEL)
    x_small = jnp.concatenate([x_sample.reshape(db, D_MODEL), meta_tokens.astype(f32),
                               jnp.zeros((SMALL_ROWS - db - N_META, D_MODEL), f32)], axis=0)
    cos_m, sin_m = _rope_tables(N_META + jnp.arange(seq))
    pos_s = jnp.concatenate([jnp.full((db,), past_len, jnp.int32), jnp.arange(N_META, dtype=jnp.int32),
                             jnp.zeros((SMALL_ROWS - db - N_META,), jnp.int32)])
    cos_s, sin_s = _rope_tables(pos_s)

    def project(x, cos, sin, tm):
        q, k, v, vb = _qkv_proj(x, g_attn, w_qkv, tm)
        cq, ckv, ckvb, kr, krb, lf, lfp = _lat_proj(x, g_attn, w_lat, qn, kvn, bias_f, cos, sin, tm)
        sg = _gate_proj(x, g_attn, w_g, tm, 1024)
        return dict(q=q, k=k, v=v, vb=vb, cq=cq, ckv=ckv, ckvb=ckvb, kr=kr, krb=krb, lf=lf, lfp=lfp, sg=sg)

    pm = project(x_main, cos_m, sin_m, 512)
    ps = project(x_small, cos_s, sin_s, SMALL_ROWS)

    meta_rows = slice(db, db + LANE)
    is_meta = (jnp.arange(LANE) < N_META)[:, None]
    lf_meta = jnp.where(is_meta, ps["lfp"][meta_rows], 0.0)[None]
    c_meta = _cumsum(lf_meta, jnp.zeros((1, LANE), f32))
    c_main = _cumsum(pm["lfp"].reshape(nb, seq, LANE), c_meta[0, N_META - 1:N_META])
    c_small = jnp.concatenate([jnp.zeros((db, LANE), f32), c_meta[0]], axis=0)

    qa_m, ka_m = _fox_pack(pm["q"], pm["k"], c_main.reshape(nb * seq, LANE), 512)
    _, ka_s = _fox_pack(ps["q"], ps["k"], c_small, SMALL_ROWS)
    zero_meta = lambda a: jnp.where(is_meta, a[meta_rows], jnp.zeros((), a.dtype))
    y_fox = _flash(qa_m, ka_m, pm["vb"], zero_meta(ka_s), zero_meta(ps["vb"]),
                   nb, seq, FOX_HEADS, FOX_GROUP, 256)
    qm_m = _mla_q(pm["cq"], wuq, cos_m, sin_m, 512, bf16)
    km_m, vm_m = _mla_kv(pm["ckvb"], pm["krb"], wuk, wuv, 512)
    km_s, vm_s = _mla_kv(ps["ckvb"], ps["krb"], wuk, wuv, SMALL_ROWS)
    y_mla = _flash(qm_m, km_m, vm_m, zero_meta(km_s), zero_meta(vm_s), nb, seq, MLA_HEADS, 1, 256)

    qm_s = _mla_q(ps["cq"], wuq, cos_s, sin_s, SMALL_ROWS, f32)
    q_lat = _mla_qlat(qm_s, wuk_t)[:db].reshape(db, MLA_HEADS, MLA_KV_LORA)
    q_rope = qm_s[:db].reshape(db, MLA_HEADS, AUG)[:, :, LANE:LANE + MLA_ROPE]
    qf = ps["q"][:db].astype(f32).reshape(db, FOX_HEADS, FOX_HEAD_DIM)
    cn = jnp.broadcast_to(ps["lf"][:db, :, None], (db, FOX_HEADS, LANE))
    kn = jnp.repeat(ps["k"][:db].reshape(db, FOX_KV_HEADS, FOX_HEAD_DIM), FOX_GROUP, axis=1)
    vn = jnp.repeat(ps["v"][:db].reshape(db, FOX_KV_HEADS, FOX_HEAD_DIM), FOX_GROUP, axis=1)
    o_fox, o_lat = _decode(
        page_table, qf, cn, kn, vn, q_lat, q_rope,
        ps["ckv"][:db].reshape(db, 1, MLA_KV_LORA), ps["kr"][:db].reshape(db, 1, MLA_ROPE),
        cache_fox_k[layer].reshape(n_pool * PAGE_SIZE * FOX_KV_HEADS, FOX_HEAD_DIM),
        cache_fox_v[layer].reshape(n_pool * PAGE_SIZE * FOX_KV_HEADS, FOX_HEAD_DIM),
        jnp.swapaxes(cache_fox_logf[layer], 1, 2), cache_mla_ckv[layer],
        jnp.swapaxes(cache_mla_krope[layer], 1, 2))
    pad_rows = lambda a: jnp.concatenate([a, jnp.zeros((SMALL_ROWS - db,) + a.shape[1:], a.dtype)], axis=0)
    y_fox_s = pad_rows(o_fox.reshape(db, FOX_Q_W).astype(bf16))
    y_mla_s = _mla_uv(pad_rows(o_lat).transpose(1, 0, 2).astype(bf16), wuv_h)

    def tail(x, yf, ym, sg, tm):
        mixed = _merge(yf, ym, sg, w_bf, w_bm, tm, 1024)
        h1 = _outproj(x, mixed, w_o, tm, 1024)
        act = _ffn_up(h1, g_ffn, w_ga, w_u, tm, 512)
        return _ffn_down(h1, act, w_d, g_fin, tm, 512)

    y_main = tail(x_main, y_fox, y_mla, pm["sg"], 512)
    y_small = tail(x_small, y_fox_s, y_mla_s, ps["sg"], SMALL_ROWS)

    def prompt_out(name, tail_shape):
        meta = jnp.broadcast_to(ps[name][db:db + N_META][None], (nb, N_META) + ps[name].shape[1:])
        full = jnp.concatenate([meta, pm[name].reshape((nb, seq) + pm[name].shape[1:])], axis=1)
        return full.reshape((1, nb, seq + N_META) + tail_shape)

    def sample_out(name, tail_shape):
        return ps[name][:db].reshape((1, db, 1) + tail_shape)

    kv_shape = (FOX_KV_HEADS, FOX_HEAD_DIM)
    return (y_main.reshape(nb, seq, D_MODEL), y_small[:db].reshape(db, 1, D_MODEL),
            prompt_out("k", kv_shape), prompt_out("v", kv_shape), prompt_out("lf", (FOX_HEADS,)),
            prompt_out("ckv", (MLA_KV_LORA,)), prompt_out("kr", (MLA_ROPE,)),
            sample_out("k", kv_shape), sample_out("v", kv_shape), sample_out("lf", (FOX_HEADS,)),
            sample_out("ckv", (MLA_KV_LORA,)), sample_out("kr", (MLA_ROPE,)))
```

```python
import functools

import numpy as np
import jax
import jax.numpy as jnp
from jax import lax
from jax.experimental import pallas as pl
from jax.experimental.pallas import tpu as pltpu

D_MODEL = 2048
N_META = 16
RMS_EPS = 1e-6
NEG_INF = -1e30
FOX_HEADS = 8
FOX_KV_HEADS = 2
FOX_GROUP = FOX_HEADS // FOX_KV_HEADS
FOX_HEAD_DIM = 128
FOX_Q_W = FOX_HEADS * FOX_HEAD_DIM
FOX_KV_W = FOX_KV_HEADS * FOX_HEAD_DIM
FOX_SCALE = FOX_HEAD_DIM ** -0.5
MLA_HEADS = 8
MLA_Q_LORA = 512
MLA_KV_LORA = 512
MLA_NOPE = 128
MLA_ROPE = 64
MLA_V = 128
MLA_SCALE = (MLA_NOPE + MLA_ROPE) ** -0.5
ROPE_BASE = 10000.0
PAGE_SIZE = 128
LANE = 128
SMALL_ROWS = 256
AUG = 2 * LANE
DEC_PAGES = 16
DEC_SUB = 4
MAIN_TM = 512
DENSE_TN = 512
ATTN_TQ = 256
ATTN_HEADS_PER_STEP = 4
VMEM_LIMIT = 56 * 1024 * 1024

bf16 = jnp.bfloat16
f32 = jnp.float32


def _cparams(sem):
    return pltpu.CompilerParams(dimension_semantics=sem, vmem_limit_bytes=VMEM_LIMIT)


def _rms(x, g):
    return x * lax.rsqrt(jnp.mean(x * x, axis=-1, keepdims=True) + RMS_EPS) * g


def _dot(a, b):
    return jnp.dot(a, b, preferred_element_type=f32)


def _dot_nt(a, b):
    return lax.dot_general(a, b, (((1,), (1,)), ((), ())), preferred_element_type=f32)


def _split3(c):
    hi = c.astype(bf16)
    r1 = c - hi.astype(f32)
    mid = r1.astype(bf16)
    lo = (r1 - mid.astype(f32)).astype(bf16)
    return hi, mid, lo


def _qkv_kernel(x_ref, g_ref, w_ref, q_ref, k_ref, v_ref, vb_ref):
    hn = _rms(x_ref[...], g_ref[...]).astype(bf16)
    z = _dot(hn, w_ref[...])
    q_ref[...] = (z[:, :FOX_Q_W] * FOX_SCALE).astype(bf16)
    k_ref[...] = z[:, FOX_Q_W:FOX_Q_W + FOX_KV_W]
    v = z[:, FOX_Q_W + FOX_KV_W:]
    v_ref[...] = v
    vb_ref[...] = v.astype(bf16)


def _qkv_proj(x, g, w, tm):
    r = x.shape[0]
    n = w.shape[1]
    row = lambda i: (i, 0)
    fix = lambda i: (0, 0)
    return pl.pallas_call(
        _qkv_kernel,
        grid=(r // tm,),
        in_specs=[pl.BlockSpec((tm, D_MODEL), row), pl.BlockSpec((1, D_MODEL), fix),
                  pl.BlockSpec((D_MODEL, n), fix)],
        out_specs=[pl.BlockSpec((tm, FOX_Q_W), row), pl.BlockSpec((tm, FOX_KV_W), row),
                   pl.BlockSpec((tm, FOX_KV_W), row), pl.BlockSpec((tm, FOX_KV_W), row)],
        out_shape=[jax.ShapeDtypeStruct((r, FOX_Q_W), bf16), jax.ShapeDtypeStruct((r, FOX_KV_W), f32),
                   jax.ShapeDtypeStruct((r, FOX_KV_W), f32), jax.ShapeDtypeStruct((r, FOX_KV_W), bf16)],
        compiler_params=_cparams(("arbitrary",)),
        name="qkv_proj",
    )(x, g, w)


def _lat_kernel(x_ref, g_ref, w_ref, qn_ref, kvn_ref, bf_ref, cos_ref, sin_ref,
                cq_ref, ckv_ref, ckvb_ref, kr_ref, krb_ref, lf_ref, lfp_ref):
    hn = _rms(x_ref[...], g_ref[...]).astype(bf16)
    z = _dot(hn, w_ref[...])
    o = 0
    cq_ref[...] = _rms(z[:, o:o + MLA_Q_LORA], qn_ref[...]).astype(bf16)
    o += MLA_Q_LORA
    ckv = _rms(z[:, o:o + MLA_KV_LORA], kvn_ref[...])
    ckv_ref[...] = ckv
    ckvb_ref[...] = ckv.astype(bf16)
    o += MLA_KV_LORA
    kr = z[:, o:o + LANE] * cos_ref[...] + z[:, o + LANE:o + 2 * LANE] * sin_ref[...]
    kr_ref[...] = kr[:, :MLA_ROPE]
    krb_ref[...] = kr.astype(bf16)
    o += 2 * LANE
    zf = z[:, o:o + LANE] + bf_ref[...]
    lf = jnp.minimum(zf, 0.0) - jnp.log(1.0 + jnp.exp(-jnp.abs(zf)))
    lane = lax.broadcasted_iota(jnp.int32, lf.shape, 1)
    lf = jnp.where(lane < FOX_HEADS, lf, 0.0)
    lf_ref[...] = lf[:, :FOX_HEADS]
    lfp_ref[...] = lf


def _lat_proj(x, g, w, qn, kvn, bfp, cos, sin, tm):
    r = x.shape[0]
    n = w.shape[1]
    ntab = cos.shape[0] // tm
    row = lambda i: (i, 0)
    fix = lambda i: (0, 0)
    tab = lambda i: (i % ntab, 0)
    return pl.pallas_call(
        _lat_kernel,
        grid=(r // tm,),
        in_specs=[pl.BlockSpec((tm, D_MODEL), row), pl.BlockSpec((1, D_MODEL), fix),
                  pl.BlockSpec((D_MODEL, n), fix), pl.BlockSpec((1, MLA_Q_LORA), fix),
                  pl.BlockSpec((1, MLA_KV_LORA), fix), pl.BlockSpec((1, LANE), fix),
                  pl.BlockSpec((tm, LANE), tab), pl.BlockSpec((tm, LANE), tab)],
        out_specs=[pl.BlockSpec((tm, MLA_Q_LORA), row), pl.BlockSpec((tm, MLA_KV_LORA), row),
                   pl.BlockSpec((tm, MLA_KV_LORA), row), pl.BlockSpec((tm, MLA_ROPE), row),
                   pl.BlockSpec((tm, LANE), row), pl.BlockSpec((tm, FOX_HEADS), row),
                   pl.BlockSpec((tm, LANE), row)],
        out_shape=[jax.ShapeDtypeStruct((r, MLA_Q_LORA), bf16), jax.ShapeDtypeStruct((r, MLA_KV_LORA), f32),
                   jax.ShapeDtypeStruct((r, MLA_KV_LORA), bf16), jax.ShapeDtypeStruct((r, MLA_ROPE), f32),
                   jax.ShapeDtypeStruct((r, LANE), bf16), jax.ShapeDtypeStruct((r, FOX_HEADS), f32),
                   jax.ShapeDtypeStruct((r, LANE), f32)],
        compiler_params=_cparams(("arbitrary",)),
        name="lat_proj",
    )(x, g, w, qn, kvn, bfp, cos, sin)


def _mla_q_kernel(cq_ref, w_ref, cos_ref, sin_ref, q_ref):
    z = _dot(cq_ref[...], w_ref[...])
    cos = cos_ref[...]
    sin = sin_ref[...]
    for h in range(MLA_HEADS):
        o = 3 * LANE * h
        q_ref[:, AUG * h:AUG * h + LANE] = (z[:, o:o + LANE] * MLA_SCALE).astype(q_ref.dtype)
        rope = z[:, o + LANE:o + 2 * LANE] * cos + z[:, o + 2 * LANE:o + 3 * LANE] * sin
        q_ref[:, AUG * h + LANE:AUG * (h + 1)] = (rope * MLA_SCALE).astype(q_ref.dtype)


def _mla_q(cq, w, cos, sin, tm, out_dtype):
    r = cq.shape[0]
    ntab = cos.shape[0] // tm
    row = lambda i: (i, 0)
    fix = lambda i: (0, 0)
    tab = lambda i: (i % ntab, 0)
    return pl.pallas_call(
        _mla_q_kernel,
        grid=(r // tm,),
        in_specs=[pl.BlockSpec((tm, MLA_Q_LORA), row), pl.BlockSpec(w.shape, fix),
                  pl.BlockSpec((tm, LANE), tab), pl.BlockSpec((tm, LANE), tab)],
        out_specs=pl.BlockSpec((tm, MLA_HEADS * AUG), row),
        out_shape=jax.ShapeDtypeStruct((r, MLA_HEADS * AUG), out_dtype),
        compiler_params=_cparams(("arbitrary",)),
        name="mla_q",
    )(cq, w, cos, sin)


def _mla_kv_kernel(ckv_ref, kr_ref, wk_ref, wv_ref, k_ref, v_ref):
    ckv = ckv_ref[...]
    kn = _dot(ckv, wk_ref[...])
    kr = kr_ref[...]
    for h in range(MLA_HEADS):
        k_ref[:, AUG * h:AUG * h + LANE] = kn[:, MLA_NOPE * h:MLA_NOPE * (h + 1)].astype(bf16)
        k_ref[:, AUG * h + LANE:AUG * (h + 1)] = kr
    v_ref[...] = _dot(ckv, wv_ref[...]).astype(bf16)


def _mla_kv(ckvb, krb, wk, wv, tm):
    r = ckvb.shape[0]
    row = lambda i: (i, 0)
    fix = lambda i: (0, 0)
    return pl.pallas_call(
        _mla_kv_kernel,
        grid=(r // tm,),
        in_specs=[pl.BlockSpec((tm, MLA_KV_LORA), row), pl.BlockSpec((tm, LANE), row),
                  pl.BlockSpec(wk.shape, fix), pl.BlockSpec(wv.shape, fix)],
        out_specs=[pl.BlockSpec((tm, MLA_HEADS * AUG), row), pl.BlockSpec((tm, MLA_HEADS * MLA_V), row)],
        out_shape=[jax.ShapeDtypeStruct((r, MLA_HEADS * AUG), bf16),
                   jax.ShapeDtypeStruct((r, MLA_HEADS * MLA_V), bf16)],
        compiler_params=_cparams(("arbitrary",)),
        name="mla_kv",
    )(ckvb, krb, wk, wv)


def _qlat_kernel(q_ref, w_ref, o_ref):
    o_ref[...] = _dot(q_ref[...], w_ref[0])


def _mla_qlat(qfull, wukt):
    r = qfull.shape[0]
    return pl.pallas_call(
        _qlat_kernel,
        grid=(MLA_HEADS,),
        in_specs=[pl.BlockSpec((r, LANE), lambda h: (0, 2 * h)),
                  pl.BlockSpec((1, MLA_NOPE, MLA_KV_LORA), lambda h: (h, 0, 0))],
        out_specs=pl.BlockSpec((r, MLA_KV_LORA), lambda h: (0, h)),
        out_shape=jax.ShapeDtypeStruct((r, MLA_HEADS * MLA_KV_LORA), f32),
        compiler_params=_cparams(("arbitrary",)),
        name="mla_qlat",
    )(qfull, wukt)


def _uv_kernel(o_ref, w_ref, y_ref):
    y_ref[...] = _dot(o_ref[0], w_ref[0]).astype(y_ref.dtype)


def _mla_uv(olat_t, wuvh):
    r = olat_t.shape[1]
    return pl.pallas_call(
        _uv_kernel,
        grid=(MLA_HEADS,),
        in_specs=[pl.BlockSpec((1, r, MLA_KV_LORA), lambda h: (h, 0, 0)),
                  pl.BlockSpec((1, MLA_KV_LORA, MLA_V), lambda h: (h, 0, 0))],
        out_specs=pl.BlockSpec((r, MLA_V), lambda h: (0, h)),
        out_shape=jax.ShapeDtypeStruct((r, MLA_HEADS * MLA_V), bf16),
        compiler_params=_cparams(("arbitrary",)),
        name="mla_uv",
    )(olat_t, wuvh)


def _cumsum_kernel(l_ref, init_ref, c_ref, carry_ref):
    @pl.when(pl.program_id(1) == 0)
    def _():
        carry_ref[...] = init_ref[...]

    blk = l_ref[0]
    r = lax.broadcasted_iota(jnp.int32, (LANE, LANE), 0)
    c = lax.broadcasted_iota(jnp.int32, (LANE, LANE), 1)
    tri = jnp.where(c <= r, 1.0, 0.0).astype(bf16)
    hi, mid, lo = _split3(blk)
    within = _dot(tri, hi) + _dot(tri, mid) + _dot(tri, lo)
    out = within + carry_ref[...]
    c_ref[0] = out
    carry_ref[...] = out[LANE - 1:LANE, :]


def _cumsum(lf, init):
    nb, t, _ = lf.shape
    return pl.pallas_call(
        _cumsum_kernel,
        grid=(nb, t // LANE),
        in_specs=[pl.BlockSpec((1, LANE, LANE), lambda b, j: (b, j, 0)),
                  pl.BlockSpec((1, LANE), lambda b, j: (0, 0))],
        out_specs=pl.BlockSpec((1, LANE, LANE), lambda b, j: (b, j, 0)),
        out_shape=jax.ShapeDtypeStruct((nb, t, LANE), f32),
        scratch_shapes=[pltpu.VMEM((1, LANE), f32)],
        compiler_params=_cparams(("arbitrary", "arbitrary")),
        name="logf_cumsum",
    )(lf, init)


def _fox_pack_kernel(q_ref, k_ref, c_ref, pq_ref, pk_ref, oq_ref, ok_ref, qa_ref, ka_ref):
    hi, mid, lo = _split3(c_ref[...])
    eq = _dot(hi, pq_ref[0]) + _dot(mid, pq_ref[1]) + _dot(lo, pq_ref[2]) + oq_ref[...]
    ek = _dot(hi, pk_ref[0]) + _dot(mid, pk_ref[1]) + _dot(lo, pk_ref[2]) + ok_ref[...]
    for h in range(FOX_HEADS):
        qa_ref[:, AUG * h:AUG * h + LANE] = q_ref[:, LANE * h:LANE * (h + 1)]
        qa_ref[:, AUG * h + LANE:AUG * (h + 1)] = eq[:, LANE * h:LANE * (h + 1)].astype(bf16)
    for g in range(FOX_KV_HEADS):
        ka_ref[:, AUG * g:AUG * g + LANE] = k_ref[:, LANE * g:LANE * (g + 1)].astype(bf16)
        ka_ref[:, AUG * g + LANE:AUG * (g + 1)] = ek[:, LANE * g:LANE * (g + 1)].astype(bf16)


def _fox_pack(q, k, c, tm):
    r = q.shape[0]
    pq = np.zeros((3, LANE, FOX_HEADS * LANE), np.float32)
    oq = np.zeros((1, FOX_HEADS * LANE), np.float32)
    pk = np.zeros((3, LANE, FOX_KV_HEADS * LANE), np.float32)
    ok = np.zeros((1, FOX_KV_HEADS * LANE), np.float32)
    for h in range(FOX_HEADS):
        g, rr = divmod(h, FOX_GROUP)
        for x in range(3):
            pq[x, h, h * LANE + x] = 1.0
            oq[0, h * LANE + 3 + 3 * rr + x] = 1.0
            pk[x, h, g * LANE + 3 + 3 * rr + x] = -1.0
    for g in range(FOX_KV_HEADS):
        ok[0, g * LANE:g * LANE + 3] = 1.0
    row = lambda i: (i, 0)
    fix2 = lambda i: (0, 0)
    fix3 = lambda i: (0, 0, 0)
    return pl.pallas_call(
        _fox_pack_kernel,
        grid=(r // tm,),
        in_specs=[pl.BlockSpec((tm, FOX_Q_W), row), pl.BlockSpec((tm, FOX_KV_W), row),
                  pl.BlockSpec((tm, LANE), row), pl.BlockSpec(pq.shape, fix3), pl.BlockSpec(pk.shape, fix3),
                  pl.BlockSpec(oq.shape, fix2), pl.BlockSpec(ok.shape, fix2)],
        out_specs=[pl.BlockSpec((tm, FOX_HEADS * AUG), row), pl.BlockSpec((tm, FOX_KV_HEADS * AUG), row)],
        out_shape=[jax.ShapeDtypeStruct((r, FOX_HEADS * AUG), bf16),
                   jax.ShapeDtypeStruct((r, FOX_KV_HEADS * AUG), bf16)],
        compiler_params=_cparams(("arbitrary",)),
        name="fox_pack",
    )(q, k, c, jnp.asarray(pq, bf16), jnp.asarray(pk, bf16), jnp.asarray(oq), jnp.asarray(ok))


def _flash_kernel(q_ref, k_ref, v_ref, km_ref, vm_ref, o_ref, *, tq, nh, shared_kv):
    qi = pl.program_id(2)
    qs = [q_ref[:, AUG * h:AUG * (h + 1)] for h in range(nh)]
    kcol = lambda h: slice(0, AUG) if shared_kv else slice(AUG * h, AUG * (h + 1))
    vcol = lambda h: slice(0, LANE) if shared_kv else slice(LANE * h, LANE * (h + 1))

    def all_heads(kref, vref, rows, carries, mask):
        heads = range(nh)
        s = [_dot_nt(qs[h], kref[rows, kcol(h)]) for h in heads]
        if mask is not None:
            s = [jnp.where(mask, s[h], NEG_INF) for h in heads]
        m_new = [jnp.maximum(carries[h][0], jnp.max(s[h], axis=-1, keepdims=True)) for h in heads]
        alpha = [jnp.exp(carries[h][0] - m_new[h]) for h in heads]
        p = [jnp.exp(s[h] - m_new[h]) for h in heads]
        l = [alpha[h] * carries[h][1] + jnp.sum(p[h], axis=-1, keepdims=True) for h in heads]
        pv = [_dot(p[h].astype(bf16), vref[rows, vcol(h)]) for h in heads]
        return tuple((m_new[h], l[h], alpha[h] * carries[h][2] + pv[h]) for h in heads)

    init = (jnp.full((tq, 1), NEG_INF, f32), jnp.zeros((tq, 1), f32), jnp.zeros((tq, LANE), f32))
    meta_col = lax.broadcasted_iota(jnp.int32, (tq, LANE), 1)
    carries = all_heads(km_ref, vm_ref, slice(None), (init,) * nh, meta_col < N_META)

    def body(j, carries):
        return all_heads(k_ref, v_ref, pl.ds(pl.multiple_of(j * tq, tq), tq), carries, None)

    carries = lax.fori_loop(0, qi, body, carries)
    row = lax.broadcasted_iota(jnp.int32, (tq, tq), 0)
    col = lax.broadcasted_iota(jnp.int32, (tq, tq), 1)
    carries = all_heads(k_ref, v_ref, pl.ds(pl.multiple_of(qi * tq, tq), tq), carries, col <= row)
    for h in range(nh):
        _, l, acc = carries[h]
        o_ref[:, LANE * h:LANE * (h + 1)] = (acc / l).astype(o_ref.dtype)


def _flash(q, k, v, km, vm, nb, seq, heads, nh, shared_kv, tq):
    nq = seq // tq
    kw = AUG if shared_kv else nh * AUG
    vw = LANE if shared_kv else nh * LANE
    return pl.pallas_call(
        functools.partial(_flash_kernel, tq=tq, nh=nh, shared_kv=shared_kv),
        grid=(nb, heads // nh, nq),
        in_specs=[pl.BlockSpec((tq, nh * AUG), lambda b, g, i: (b * nq + i, g)),
                  pl.BlockSpec((seq, kw), lambda b, g, i: (b, g)),
                  pl.BlockSpec((seq, vw), lambda b, g, i: (b, g)),
                  pl.BlockSpec((LANE, kw), lambda b, g, i: (0, g)),
                  pl.BlockSpec((LANE, vw), lambda b, g, i: (0, g))],
        out_specs=pl.BlockSpec((tq, nh * LANE), lambda b, g, i: (b * nq + i, g)),
        out_shape=jax.ShapeDtypeStruct((nb * seq, heads * LANE), bf16),
        compiler_params=_cparams(("arbitrary", "arbitrary", "arbitrary")),
        name="prompt_attn",
    )(q, k, v, km, vm)


def _decode_kernel(pt_ref, qf_ref, cn_ref, kn_ref, vn_ref, ql_ref, qr_ref, cnew_ref, rnew_ref,
                   ck_hbm, cv_hbm, clf_hbm, cc_hbm, cr_hbm,
                   of_ref, ol_ref,
                   kbuf, vbuf, lfbuf, cbuf, rbuf, sems, mf, lfs, accf, mm, lms, accm, carry,
                   *, n_chunks, n_pages):
    b = pl.program_id(0)
    c = pl.program_id(1)
    nb = pl.num_programs(0)
    t = b * n_chunks + c
    slot = t % 2
    npg = DEC_PAGES
    kv_rows = PAGE_SIZE * FOX_KV_HEADS

    def copies(bb, cc, sl):
        out = []
        for p in range(npg):
            pid = pt_ref[bb, (n_chunks - 1 - cc) * npg + p]
            out.append(pltpu.make_async_copy(ck_hbm.at[pl.ds(pid * kv_rows, kv_rows), :],
                                             kbuf.at[sl, pl.ds(p * kv_rows, kv_rows), :], sems.at[sl, 0]))
            out.append(pltpu.make_async_copy(cv_hbm.at[pl.ds(pid * kv_rows, kv_rows), :],
                                             vbuf.at[sl, pl.ds(p * kv_rows, kv_rows), :], sems.at[sl, 1]))
            out.append(pltpu.make_async_copy(clf_hbm.at[pid],
                                             lfbuf.at[sl, pl.ds(p * FOX_HEADS, FOX_HEADS), :], sems.at[sl, 2]))
            out.append(pltpu.make_async_copy(cc_hbm.at[pid],
                                             cbuf.at[sl, pl.ds(p * PAGE_SIZE, PAGE_SIZE), :], sems.at[sl, 3]))
            out.append(pltpu.make_async_copy(cr_hbm.at[pid],
                                             rbuf.at[sl, :, pl.ds(p * PAGE_SIZE, PAGE_SIZE)], sems.at[sl, 4]))
        return out

    @pl.when(t == 0)
    def _():
        for cp in copies(b, c, slot):
            cp.start()

    @pl.when(c == 0)
    def _():
        mf[...] = jnp.full(mf.shape, NEG_INF, f32)
        lfs[...] = jnp.zeros(lfs.shape, f32)
        accf[...] = jnp.zeros(accf.shape, f32)
        mm[...] = jnp.full(mm.shape, NEG_INF, f32)
        lms[...] = jnp.zeros(lms.shape, f32)
        accm[...] = jnp.zeros(accm.shape, f32)
        carry[...] = jnp.zeros(carry.shape, f32)

    for cp in copies(b, c, slot):
        cp.wait()

    t_last = nb * n_chunks - 1
    t_next = jnp.minimum(t + 1, t_last)
    for cp in copies(t_next // n_chunks, t_next % n_chunks, 1 - slot):
        cp.start()

    qf = qf_ref[0]
    stacked = lfbuf[slot]
    r = lax.broadcasted_iota(jnp.int32, (PAGE_SIZE, kv_rows), 0)
    cidx = lax.broadcasted_iota(jnp.int32, (PAGE_SIZE, kv_rows), 1)
    tri = jnp.where(r > cidx // FOX_KV_HEADS, 1.0, 0.0).astype(bf16)
    pieces = [stacked[p * FOX_HEADS:(p + 1) * FOX_HEADS, :] for p in range(npg)]
    hi, mid, lo = _split3(stacked)
    within = _dot(hi, tri) + _dot(mid, tri) + _dot(lo, tri)
    run = carry[:, :1]
    bias = [None] * npg
    for p in reversed(range(npg)):
        bias[p] = within[p * FOX_HEADS:(p + 1) * FOX_HEADS, :] + run
        run = run + jnp.sum(pieces[p], axis=-1, keepdims=True)
    carry[...] = jnp.broadcast_to(run, carry.shape)
    def softmax_piece(s):
        m = jnp.max(s, axis=-1, keepdims=True)
        p = jnp.exp(s - m)
        return m, jnp.sum(p, axis=-1, keepdims=True), p

    def fold(parts, m_ref, l_ref, acc_ref):
        m_old = m_ref[:, :1]
        m_new = m_old
        for m, _, _ in parts:
            m_new = jnp.maximum(m_new, m)
        w_old = jnp.exp(m_old - m_new)
        l = w_old * l_ref[...]
        acc = w_old * acc_ref[...]
        for m, lp, ap in parts:
            w = jnp.exp(m - m_new)
            l = l + w * lp
            acc = acc + w * ap
        m_ref[...] = jnp.broadcast_to(m_new, m_ref.shape)
        l_ref[...] = l
        acc_ref[...] = acc

    pps = npg // DEC_SUB
    sub_kv = pps * kv_rows
    head = lax.broadcasted_iota(jnp.int32, (FOX_HEADS, sub_kv), 0)
    kvrow = lax.broadcasted_iota(jnp.int32, (FOX_HEADS, sub_kv), 1)
    own_kv = head // FOX_GROUP == kvrow % FOX_KV_HEADS
    cn = cn_ref[0][:, :1]
    sub_k = pps * PAGE_SIZE
    kv_blk = lambda i: pl.ds(i * sub_kv, sub_kv)
    k_blk = lambda i: pl.ds(i * sub_k, sub_k)
    subs = range(DEC_SUB)
    s_fox = [_dot_nt(qf, kbuf[slot, kv_blk(i), :]) for i in subs]
    s_mla = [_dot_nt(ql_ref[0], cbuf[slot, k_blk(i), :]) + _dot(qr_ref[0], rbuf[slot, :, k_blk(i)])
             for i in subs]
    p_fox = [softmax_piece(jnp.where(
        own_kv, s_fox[i] + jnp.concatenate(bias[i * pps:(i + 1) * pps], axis=1) + cn, NEG_INF)) for i in subs]
    p_mla = [softmax_piece(s_mla[i]) for i in subs]
    a_fox = [_dot(p_fox[i][2], vbuf[slot, kv_blk(i), :]) for i in subs]
    a_mla = [_dot(p_mla[i][2], cbuf[slot, k_blk(i), :]) for i in subs]
    fold([(p_fox[i][0], p_fox[i][1], a_fox[i]) for i in subs], mf, lfs, accf)
    fold([(p_mla[i][0], p_mla[i][1], a_mla[i]) for i in subs], mm, lms, accm)

    @pl.when(c == n_chunks - 1)
    def _():
        s_self = jnp.sum(qf * kn_ref[0], axis=-1, keepdims=True)
        m_old = mf[:, :1]
        m_new = jnp.maximum(m_old, s_self)
        alpha = jnp.exp(m_old - m_new)
        p_self = jnp.exp(s_self - m_new)
        l = alpha * lfs[:, :1] + p_self
        of_ref[0] = (alpha * accf[...] + p_self * vn_ref[0]) / l

        cnew = cnew_ref[0]
        s_self = (jnp.sum(ql_ref[0] * cnew, axis=-1, keepdims=True)
                  + jnp.sum(qr_ref[0] * rnew_ref[0], axis=-1, keepdims=True))
        m_old = mm[:, :1]
        m_new = jnp.maximum(m_old, s_self)
        alpha = jnp.exp(m_old - m_new)
        p_self = jnp.exp(s_self - m_new)
        l = alpha * lms[:, :1] + p_self
        ol_ref[0] = (alpha * accm[...] + p_self * cnew) / l

    @pl.when(t == t_last)
    def _():
        for cp in copies(b, c, 1 - slot):
            cp.wait()


def _decode(page_table, qf, cn, kn, vn, ql, qr, cnew, rnew, ck, cv, clf, cc, cr):
    nb, n_pages = page_table.shape
    n_chunks = n_pages // DEC_PAGES
    kv_rows = PAGE_SIZE * FOX_KV_HEADS
    per_b = lambda shape: pl.BlockSpec((1,) + shape, lambda b, c, pt: (b, 0, 0))
    any_spec = pl.BlockSpec(memory_space=pl.ANY)
    grid_spec = pltpu.PrefetchScalarGridSpec(
        num_scalar_prefetch=1,
        grid=(nb, n_chunks),
        in_specs=[per_b((FOX_HEADS, FOX_HEAD_DIM)), per_b((FOX_HEADS, LANE)),
                  per_b((FOX_HEADS, FOX_HEAD_DIM)), per_b((FOX_HEADS, FOX_HEAD_DIM)),
                  per_b((MLA_HEADS, MLA_KV_LORA)), per_b((MLA_HEADS, MLA_ROPE)),
                  per_b((1, MLA_KV_LORA)), per_b((1, MLA_ROPE)),
                  any_spec, any_spec, any_spec, any_spec, any_spec],
        out_specs=[per_b((FOX_HEADS, FOX_HEAD_DIM)), per_b((MLA_HEADS, MLA_KV_LORA))],
        scratch_shapes=[
            pltpu.VMEM((2, DEC_PAGES * kv_rows, FOX_HEAD_DIM), f32),
            pltpu.VMEM((2, DEC_PAGES * kv_rows, FOX_HEAD_DIM), f32),
            pltpu.VMEM((2, DEC_PAGES * FOX_HEADS, PAGE_SIZE), f32),
            pltpu.VMEM((2, DEC_PAGES * PAGE_SIZE, MLA_KV_LORA), f32),
            pltpu.VMEM((2, MLA_ROPE, DEC_PAGES * PAGE_SIZE), f32),
            pltpu.SemaphoreType.DMA((2, 5)),
            pltpu.VMEM((FOX_HEADS, LANE), f32), pltpu.VMEM((FOX_HEADS, LANE), f32),
            pltpu.VMEM((FOX_HEADS, FOX_HEAD_DIM), f32),
            pltpu.VMEM((MLA_HEADS, LANE), f32), pltpu.VMEM((MLA_HEADS, LANE), f32),
            pltpu.VMEM((MLA_HEADS, MLA_KV_LORA), f32),
            pltpu.VMEM((FOX_HEADS, LANE), f32),
        ],
    )
    return pl.pallas_call(
        functools.partial(_decode_kernel, n_chunks=n_chunks, n_pages=n_pages),
        grid_spec=grid_spec,
        out_shape=[jax.ShapeDtypeStruct((nb, FOX_HEADS, FOX_HEAD_DIM), f32),
                   jax.ShapeDtypeStruct((nb, MLA_HEADS, MLA_KV_LORA), f32)],
        compiler_params=_cparams(("arbitrary", "arbitrary")),
        name="paged_decode_attn",
    )(page_table, qf, cn, kn, vn, ql, qr, cnew, rnew, ck, cv, clf, cc, cr)


def _attn_tail_kernel(x_ref, g_ref, yf_ref, ym_ref, wga_ref, wgb_ref, wf_ref, wm_ref, wo_ref, o_ref,
                      hn_ref, xs_ref, mix_ref, *, nj, tn):
    j = pl.program_id(1)

    @pl.when(j == 0)
    def _():
        x = x_ref[...]
        hn_ref[...] = _rms(x, g_ref[...]).astype(bf16)
        for c in range(nj):
            xs_ref[c] = x[:, c * tn:(c + 1) * tn]

    @pl.when(j < nj)
    def _():
        hn = hn_ref[...]
        ga = 1.0 / (1.0 + jnp.exp(-_dot(hn, wga_ref[...])))
        gb = 1.0 / (1.0 + jnp.exp(-_dot(hn, wgb_ref[...])))
        mix = ga * _dot(yf_ref[...], wf_ref[...]) + gb * _dot(ym_ref[...], wm_ref[...])
        mix_ref[j] = mix.astype(bf16)

    @pl.when(j >= nj)
    def _():
        acc = xs_ref[j - nj]
        for c in range(nj):
            acc = acc + _dot(mix_ref[c], wo_ref[c * tn:(c + 1) * tn, :])
        o_ref[...] = acc


def _attn_tail(x, g, yf, ym, wg, wf, wm, wo, tm, tn):
    r = x.shape[0]
    nj = D_MODEL // tn
    mix_col = lambda i, j: (0, jnp.minimum(j, nj - 1))
    out_col = lambda i, j: (0, jnp.maximum(j - nj, 0))
    return pl.pallas_call(
        functools.partial(_attn_tail_kernel, nj=nj, tn=tn),
        grid=(r // tm, 2 * nj),
        in_specs=[pl.BlockSpec((tm, D_MODEL), lambda i, j: (i, 0)),
                  pl.BlockSpec((1, D_MODEL), lambda i, j: (0, 0)),
                  pl.BlockSpec((tm, FOX_Q_W), lambda i, j: (i, 0)),
                  pl.BlockSpec((tm, MLA_HEADS * MLA_V), lambda i, j: (i, 0)),
                  pl.BlockSpec((D_MODEL, tn), mix_col),
                  pl.BlockSpec((D_MODEL, tn), lambda i, j: (0, nj + jnp.minimum(j, nj - 1))),
                  pl.BlockSpec((FOX_Q_W, tn), mix_col),
                  pl.BlockSpec((MLA_HEADS * MLA_V, tn), mix_col),
                  pl.BlockSpec((D_MODEL, tn), out_col)],
        out_specs=pl.BlockSpec((tm, tn), lambda i, j: (i, jnp.maximum(j - nj, 0))),
        out_shape=jax.ShapeDtypeStruct((r, D_MODEL), f32),
        scratch_shapes=[pltpu.VMEM((tm, D_MODEL), bf16), pltpu.VMEM((nj, tm, tn), f32),
                        pltpu.VMEM((nj, tm, tn), bf16)],
        compiler_params=_cparams(("arbitrary", "arbitrary")),
        name="attn_tail",
    )(x, g, yf, ym, wg, wg, wf, wm, wo)


def _ffn_kernel(h_ref, g_ref, wg_ref, wu_ref, wd_ref, gf_ref, y_ref, hn_ref, acc_ref):
    j = pl.program_id(1)

    @pl.when(j == 0)
    def _():
        h = h_ref[...]
        hn_ref[...] = _rms(h, g_ref[...]).astype(bf16)
        acc_ref[...] = h

    hn = hn_ref[...]
    a = _dot(hn, wg_ref[...])
    u = _dot(hn, wu_ref[...])
    act = (a / (1.0 + jnp.exp(-a)) * u).astype(bf16)
    acc_ref[...] += _dot(act, wd_ref[...])

    @pl.when(j == pl.num_programs(1) - 1)
    def _():
        y_ref[...] = _rms(acc_ref[...], gf_ref[...])


def _ffn(h, g, wg, wu, wd, gf, tm, tf):
    r = h.shape[0]
    nf = wg.shape[1] // tf
    return pl.pallas_call(
        _ffn_kernel,
        grid=(r // tm, nf),
        in_specs=[pl.BlockSpec((tm, D_MODEL), lambda i, j: (i, 0)),
                  pl.BlockSpec((1, D_MODEL), lambda i, j: (0, 0)),
                  pl.BlockSpec((D_MODEL, tf), lambda i, j: (0, j)),
                  pl.BlockSpec((D_MODEL, tf), lambda i, j: (0, j)),
                  pl.BlockSpec((tf, D_MODEL), lambda i, j: (j, 0)),
                  pl.BlockSpec((1, D_MODEL), lambda i, j: (0, 0))],
        out_specs=pl.BlockSpec((tm, D_MODEL), lambda i, j: (i, 0)),
        out_shape=jax.ShapeDtypeStruct((r, D_MODEL), f32),
        scratch_shapes=[pltpu.VMEM((tm, D_MODEL), bf16), pltpu.VMEM((tm, D_MODEL), f32)],
        compiler_params=_cparams(("arbitrary", "arbitrary")),
        name="ffn",
    )(h, g, wg, wu, wd, gf)


def _rope_tables(pos):
    inv_freq = ROPE_BASE ** (-jnp.arange(0, MLA_ROPE, 2, dtype=f32) / MLA_ROPE)
    ang = pos.astype(f32)[:, None] * inv_freq[None, :]
    c, s = jnp.cos(ang), jnp.sin(ang)
    z = jnp.zeros((pos.shape[0], LANE - MLA_ROPE), f32)
    return jnp.concatenate([c, c, z], axis=1), jnp.concatenate([-s, s, z], axis=1)


def _swap_halves(w):
    half = w.shape[-1] // 2
    return jnp.concatenate([w[..., half:], w[..., :half]], axis=-1)


def kernel(x_prompt, x_sample, cache_fox_k, cache_fox_v, cache_fox_logf, cache_mla_ckv, cache_mla_krope,
           page_table, meta_tokens, attn_norm, w_in, fox_forget_bias, mla_q_norm, mla_kv_norm, mla_w_uq,
           mla_w_uk, mla_w_uv, w_branch_fox, w_branch_mla, w_out, ffn_norm, w_gate, w_up, w_down, final_norm):
    nb, seq, _ = x_prompt.shape
    db = x_sample.shape[0]
    n_pool = cache_fox_k.shape[1]
    past_len = page_table.shape[1] * PAGE_SIZE
    layer = 0

    w0 = w_in[layer]
    o_f = FOX_Q_W + 2 * FOX_KV_W
    o_cq = o_f + FOX_HEADS
    o_ckv = o_cq + MLA_Q_LORA
    o_kr = o_ckv + MLA_KV_LORA
    o_g = o_kr + MLA_ROPE
    w_qkv = w0[:, :o_f].astype(bf16)
    w_kr = w0[:, o_kr:o_g]
    zpad = lambda n: jnp.zeros((D_MODEL, n), f32)
    w_lat = jnp.concatenate([w0[:, o_cq:o_kr], w_kr, zpad(LANE - MLA_ROPE), _swap_halves(w_kr),
                             zpad(LANE - MLA_ROPE), w0[:, o_f:o_cq], zpad(LANE - FOX_HEADS)], axis=1).astype(bf16)
    w_g = w0[:, o_g:].astype(bf16)
    bias_f = jnp.concatenate([fox_forget_bias[layer], jnp.zeros((LANE - FOX_HEADS,), f32)])[None]
    wuq = mla_w_uq[layer].reshape(MLA_Q_LORA, MLA_HEADS, MLA_NOPE + MLA_ROPE)
    zq = jnp.zeros((MLA_Q_LORA, MLA_HEADS, LANE - MLA_ROPE), f32)
    wuq = jnp.concatenate([wuq[..., :MLA_NOPE], wuq[..., MLA_NOPE:], zq,
                           _swap_halves(wuq[..., MLA_NOPE:]), zq], axis=-1)
    wuq = wuq.reshape(MLA_Q_LORA, MLA_HEADS * 3 * LANE).astype(bf16)
    wuk = mla_w_uk[layer].reshape(MLA_KV_LORA, MLA_HEADS * MLA_NOPE).astype(bf16)
    wuv = mla_w_uv[layer].reshape(MLA_KV_LORA, MLA_HEADS * MLA_V).astype(bf16)
    wuk_t = mla_w_uk[layer].transpose(1, 2, 0)
    wuv_h = mla_w_uv[layer].transpose(1, 0, 2).astype(bf16)
    w_bf = w_branch_fox[layer].astype(bf16)
    w_bm = w_branch_mla[layer].astype(bf16)
    w_o = w_out[layer].astype(bf16)
    w_ga = w_gate[layer].astype(bf16)
    w_u = w_up[layer].astype(bf16)
    w_d = w_down[layer].astype(bf16)
    g_attn = attn_norm[layer][None]
    g_ffn = ffn_norm[layer][None]
    g_fin = final_norm[None]
    qn = mla_q_norm[layer][None]
    kvn = mla_kv_norm[layer][None]

    x_main = x_prompt.reshape(nb * seq, D_MODEL)
    x_small = jnp.concatenate([x_sample.reshape(db, D_MODEL), meta_tokens.astype(f32),
                               jnp.zeros((SMALL_ROWS - db - N_META, D_MODEL), f32)], axis=0)
    cos_m, sin_m = _rope_tables(N_META + jnp.arange(seq))
    pos_s = jnp.concatenate([jnp.full((db,), past_len, jnp.int32), jnp.arange(N_META, dtype=jnp.int32),
                             jnp.zeros((SMALL_ROWS - db - N_META,), jnp.int32)])
    cos_s, sin_s = _rope_tables(pos_s)

    def project(x, cos, sin, tm):
        q, k, v, vb = _qkv_proj(x, g_attn, w_qkv, tm)
        cq, ckv, ckvb, kr, krb, lf, lfp = _lat_proj(x, g_attn, w_lat, qn, kvn, bias_f, cos, sin, tm)
        return dict(q=q, k=k, v=v, vb=vb, cq=cq, ckv=ckv, ckvb=ckvb, kr=kr, krb=krb, lf=lf, lfp=lfp)

    pm = project(x_main, cos_m, sin_m, MAIN_TM)
    ps = project(x_small, cos_s, sin_s, SMALL_ROWS)

    meta_rows = slice(db, db + LANE)
    is_meta = (jnp.arange(LANE) < N_META)[:, None]
    lf_meta = jnp.where(is_meta, ps["lfp"][meta_rows], 0.0)[None]
    c_meta = _cumsum(lf_meta, jnp.zeros((1, LANE), f32))
    c_main = _cumsum(pm["lfp"].reshape(nb, seq, LANE), c_meta[0, N_META - 1:N_META])
    c_small = jnp.concatenate([jnp.zeros((db, LANE), f32), c_meta[0]], axis=0)

    qa_m, ka_m = _fox_pack(pm["q"], pm["k"], c_main.reshape(nb * seq, LANE), MAIN_TM)
    _, ka_s = _fox_pack(ps["q"], ps["k"], c_small, SMALL_ROWS)
    zero_meta = lambda a: jnp.where(is_meta, a[meta_rows], jnp.zeros((), a.dtype))
    y_fox = _flash(qa_m, ka_m, pm["vb"], zero_meta(ka_s), zero_meta(ps["vb"]),
                   nb, seq, FOX_HEADS, FOX_GROUP, True, ATTN_TQ)
    qm_m = _mla_q(pm["cq"], wuq, cos_m, sin_m, MAIN_TM, bf16)
    km_m, vm_m = _mla_kv(pm["ckvb"], pm["krb"], wuk, wuv, MAIN_TM)
    km_s, vm_s = _mla_kv(ps["ckvb"], ps["krb"], wuk, wuv, SMALL_ROWS)
    y_mla = _flash(qm_m, km_m, vm_m, zero_meta(km_s), zero_meta(vm_s),
                   nb, seq, MLA_HEADS, ATTN_HEADS_PER_STEP, False, ATTN_TQ)

    qm_s = _mla_q(ps["cq"], wuq, cos_s, sin_s, SMALL_ROWS, f32)
    q_lat = _mla_qlat(qm_s, wuk_t)[:db].reshape(db, MLA_HEADS, MLA_KV_LORA)
    q_rope = qm_s[:db].reshape(db, MLA_HEADS, AUG)[:, :, LANE:LANE + MLA_ROPE]
    qf = ps["q"][:db].astype(f32).reshape(db, FOX_HEADS, FOX_HEAD_DIM)
    cn = jnp.broadcast_to(ps["lf"][:db, :, None], (db, FOX_HEADS, LANE))
    kn = jnp.repeat(ps["k"][:db].reshape(db, FOX_KV_HEADS, FOX_HEAD_DIM), FOX_GROUP, axis=1)
    vn = jnp.repeat(ps["v"][:db].reshape(db, FOX_KV_HEADS, FOX_HEAD_DIM), FOX_GROUP, axis=1)
    o_fox, o_lat = _decode(
        page_table, qf, cn, kn, vn, q_lat, q_rope,
        ps["ckv"][:db].reshape(db, 1, MLA_KV_LORA), ps["kr"][:db].reshape(db, 1, MLA_ROPE),
        cache_fox_k[layer].reshape(n_pool * PAGE_SIZE * FOX_KV_HEADS, FOX_HEAD_DIM),
        cache_fox_v[layer].reshape(n_pool * PAGE_SIZE * FOX_KV_HEADS, FOX_HEAD_DIM),
        jnp.swapaxes(cache_fox_logf[layer], 1, 2), cache_mla_ckv[layer],
        jnp.swapaxes(cache_mla_krope[layer], 1, 2))
    pad_rows = lambda a: jnp.concatenate([a, jnp.zeros((SMALL_ROWS - db,) + a.shape[1:], a.dtype)], axis=0)
    y_fox_s = pad_rows(o_fox.reshape(db, FOX_Q_W).astype(bf16))
    y_mla_s = _mla_uv(pad_rows(o_lat).transpose(1, 0, 2).astype(bf16), wuv_h)

    def tail(x, yf, ym, tm):
        h1 = _attn_tail(x, g_attn, yf, ym, w_g, w_bf, w_bm, w_o, tm, DENSE_TN)
        return _ffn(h1, g_ffn, w_ga, w_u, w_d, g_fin, tm, DENSE_TN)

    y_main = tail(x_main, y_fox, y_mla, MAIN_TM)
    y_small = tail(x_small, y_fox_s, y_mla_s, SMALL_ROWS)

    def prompt_out(name, tail_shape):
        meta = jnp.broadcast_to(ps[name][db:db + N_META][None], (nb, N_META) + ps[name].shape[1:])
        full = jnp.concatenate([meta, pm[name].reshape((nb, seq) + pm[name].shape[1:])], axis=1)
        return full.reshape((1, nb, seq + N_META) + tail_shape)

    def sample_out(name, tail_shape):
        return ps[name][:db].reshape((1, db, 1) + tail_shape)

    kv_shape = (FOX_KV_HEADS, FOX_HEAD_DIM)
    return (y_main.reshape(nb, seq, D_MODEL), y_small[:db].reshape(db, 1, D_MODEL),
            prompt_out("k", kv_shape), prompt_out("v", kv_shape), prompt_out("lf", (FOX_HEADS,)),
            prompt_out("ckv", (MLA_KV_LORA,)), prompt_out("kr", (MLA_ROPE,)),
            sample_out("k", kv_shape), sample_out("v", kv_shape), sample_out("lf", (FOX_HEADS,)),
            sample_out("ckv", (MLA_KV_LORA,)), sample_out("kr", (MLA_ROPE,)))
```

```python
import functools

import numpy as np
import jax
import jax.numpy as jnp
from jax import lax
from jax.experimental import pallas as pl
from jax.experimental.pallas import tpu as pltpu

D_MODEL = 2048
N_META = 16
RMS_EPS = 1e-6
NEG_INF = -1e30
FOX_HEADS = 8
FOX_KV_HEADS = 2
FOX_GROUP = FOX_HEADS // FOX_KV_HEADS
FOX_HEAD_DIM = 128
FOX_Q_W = FOX_HEADS * FOX_HEAD_DIM
FOX_KV_W = FOX_KV_HEADS * FOX_HEAD_DIM
FOX_SCALE = FOX_HEAD_DIM ** -0.5
MLA_HEADS = 8
MLA_Q_LORA = 512
MLA_KV_LORA = 512
MLA_NOPE = 128
MLA_ROPE = 64
MLA_V = 128
MLA_SCALE = (MLA_NOPE + MLA_ROPE) ** -0.5
ROPE_BASE = 10000.0
PAGE_SIZE = 128
LANE = 128
SMALL_ROWS = 256
AUG = 2 * LANE
DEC_PAGES = 16
DEC_SUB = 4
DEC_SLOTS = 3
MAIN_TM = 512
DENSE_TN = 512
ATTN_TQ = 512
ATTN_HEADS_PER_STEP = 2
VMEM_LIMIT = 56 * 1024 * 1024

bf16 = jnp.bfloat16
f32 = jnp.float32


def _cparams(sem):
    return pltpu.CompilerParams(dimension_semantics=sem, vmem_limit_bytes=VMEM_LIMIT)


def _rms(x, g):
    return x * lax.rsqrt(jnp.mean(x * x, axis=-1, keepdims=True) + RMS_EPS) * g


def _dot(a, b):
    return jnp.dot(a, b, preferred_element_type=f32)


def _dot_nt(a, b):
    return lax.dot_general(a, b, (((1,), (1,)), ((), ())), preferred_element_type=f32)


def _split3(c):
    hi = c.astype(bf16)
    r1 = c - hi.astype(f32)
    mid = r1.astype(bf16)
    lo = (r1 - mid.astype(f32)).astype(bf16)
    return hi, mid, lo


def _qkv_kernel(x_ref, g_ref, w_ref, q_ref, k_ref, v_ref, vb_ref):
    hn = _rms(x_ref[...], g_ref[...]).astype(bf16)
    z = _dot(hn, w_ref[...])
    q_ref[...] = (z[:, :FOX_Q_W] * FOX_SCALE).astype(bf16)
    k_ref[...] = z[:, FOX_Q_W:FOX_Q_W + FOX_KV_W]
    v = z[:, FOX_Q_W + FOX_KV_W:]
    v_ref[...] = v
    vb_ref[...] = v.astype(bf16)


def _qkv_proj(x, g, w, tm):
    r = x.shape[0]
    n = w.shape[1]
    row = lambda i: (i, 0)
    fix = lambda i: (0, 0)
    return pl.pallas_call(
        _qkv_kernel,
        grid=(r // tm,),
        in_specs=[pl.BlockSpec((tm, D_MODEL), row), pl.BlockSpec((1, D_MODEL), fix),
                  pl.BlockSpec((D_MODEL, n), fix)],
        out_specs=[pl.BlockSpec((tm, FOX_Q_W), row), pl.BlockSpec((tm, FOX_KV_W), row),
                   pl.BlockSpec((tm, FOX_KV_W), row), pl.BlockSpec((tm, FOX_KV_W), row)],
        out_shape=[jax.ShapeDtypeStruct((r, FOX_Q_W), bf16), jax.ShapeDtypeStruct((r, FOX_KV_W), f32),
                   jax.ShapeDtypeStruct((r, FOX_KV_W), f32), jax.ShapeDtypeStruct((r, FOX_KV_W), bf16)],
        compiler_params=_cparams(("arbitrary",)),
        name="qkv_proj",
    )(x, g, w)


def _lat_kernel(x_ref, g_ref, w_ref, qn_ref, kvn_ref, bf_ref, cos_ref, sin_ref,
                cq_ref, ckv_ref, ckvb_ref, kr_ref, krb_ref, lf_ref, lfp_ref):
    hn = _rms(x_ref[...], g_ref[...]).astype(bf16)
    z = _dot(hn, w_ref[...])
    o = 0
    cq_ref[...] = _rms(z[:, o:o + MLA_Q_LORA], qn_ref[...]).astype(bf16)
    o += MLA_Q_LORA
    ckv = _rms(z[:, o:o + MLA_KV_LORA], kvn_ref[...])
    ckv_ref[...] = ckv
    ckvb_ref[...] = ckv.astype(bf16)
    o += MLA_KV_LORA
    kr = z[:, o:o + LANE] * cos_ref[...] + z[:, o + LANE:o + 2 * LANE] * sin_ref[...]
    kr_ref[...] = kr[:, :MLA_ROPE]
    krb_ref[...] = kr.astype(bf16)
    o += 2 * LANE
    zf = z[:, o:o + LANE] + bf_ref[...]
    lf = jnp.minimum(zf, 0.0) - jnp.log(1.0 + jnp.exp(-jnp.abs(zf)))
    lane = lax.broadcasted_iota(jnp.int32, lf.shape, 1)
    lf = jnp.where(lane < FOX_HEADS, lf, 0.0)
    lf_ref[...] = lf[:, :FOX_HEADS]
    lfp_ref[...] = lf


def _lat_proj(x, g, w, qn, kvn, bfp, cos, sin, tm):
    r = x.shape[0]
    n = w.shape[1]
    ntab = cos.shape[0] // tm
    row = lambda i: (i, 0)
    fix = lambda i: (0, 0)
    tab = lambda i: (i % ntab, 0)
    return pl.pallas_call(
        _lat_kernel,
        grid=(r // tm,),
        in_specs=[pl.BlockSpec((tm, D_MODEL), row), pl.BlockSpec((1, D_MODEL), fix),
                  pl.BlockSpec((D_MODEL, n), fix), pl.BlockSpec((1, MLA_Q_LORA), fix),
                  pl.BlockSpec((1, MLA_KV_LORA), fix), pl.BlockSpec((1, LANE), fix),
                  pl.BlockSpec((tm, LANE), tab), pl.BlockSpec((tm, LANE), tab)],
        out_specs=[pl.BlockSpec((tm, MLA_Q_LORA), row), pl.BlockSpec((tm, MLA_KV_LORA), row),
                   pl.BlockSpec((tm, MLA_KV_LORA), row), pl.BlockSpec((tm, MLA_ROPE), row),
                   pl.BlockSpec((tm, LANE), row), pl.BlockSpec((tm, FOX_HEADS), row),
                   pl.BlockSpec((tm, LANE), row)],
        out_shape=[jax.ShapeDtypeStruct((r, MLA_Q_LORA), bf16), jax.ShapeDtypeStruct((r, MLA_KV_LORA), f32),
                   jax.ShapeDtypeStruct((r, MLA_KV_LORA), bf16), jax.ShapeDtypeStruct((r, MLA_ROPE), f32),
                   jax.ShapeDtypeStruct((r, LANE), bf16), jax.ShapeDtypeStruct((r, FOX_HEADS), f32),
                   jax.ShapeDtypeStruct((r, LANE), f32)],
        compiler_params=_cparams(("arbitrary",)),
        name="lat_proj",
    )(x, g, w, qn, kvn, bfp, cos, sin)


def _mla_q_kernel(cq_ref, w_ref, cos_ref, sin_ref, q_ref):
    z = _dot(cq_ref[...], w_ref[...])
    cos = cos_ref[...]
    sin = sin_ref[...]
    for h in range(MLA_HEADS):
        o = 3 * LANE * h
        q_ref[:, AUG * h:AUG * h + LANE] = (z[:, o:o + LANE] * MLA_SCALE).astype(q_ref.dtype)
        rope = z[:, o + LANE:o + 2 * LANE] * cos + z[:, o + 2 * LANE:o + 3 * LANE] * sin
        q_ref[:, AUG * h + LANE:AUG * (h + 1)] = (rope * MLA_SCALE).astype(q_ref.dtype)


def _mla_q(cq, w, cos, sin, tm, out_dtype):
    r = cq.shape[0]
    ntab = cos.shape[0] // tm
    row = lambda i: (i, 0)
    fix = lambda i: (0, 0)
    tab = lambda i: (i % ntab, 0)
    return pl.pallas_call(
        _mla_q_kernel,
        grid=(r // tm,),
        in_specs=[pl.BlockSpec((tm, MLA_Q_LORA), row), pl.BlockSpec(w.shape, fix),
                  pl.BlockSpec((tm, LANE), tab), pl.BlockSpec((tm, LANE), tab)],
        out_specs=pl.BlockSpec((tm, MLA_HEADS * AUG), row),
        out_shape=jax.ShapeDtypeStruct((r, MLA_HEADS * AUG), out_dtype),
        compiler_params=_cparams(("arbitrary",)),
        name="mla_q",
    )(cq, w, cos, sin)


def _mla_kv_kernel(ckv_ref, kr_ref, wk_ref, wv_ref, k_ref, v_ref):
    ckv = ckv_ref[...]
    kn = _dot(ckv, wk_ref[...])
    kr = kr_ref[...]
    for h in range(MLA_HEADS):
        k_ref[:, AUG * h:AUG * h + LANE] = kn[:, MLA_NOPE * h:MLA_NOPE * (h + 1)].astype(bf16)
        k_ref[:, AUG * h + LANE:AUG * (h + 1)] = kr
    v_ref[...] = _dot(ckv, wv_ref[...]).astype(bf16)


def _mla_kv(ckvb, krb, wk, wv, tm):
    r = ckvb.shape[0]
    row = lambda i: (i, 0)
    fix = lambda i: (0, 0)
    return pl.pallas_call(
        _mla_kv_kernel,
        grid=(r // tm,),
        in_specs=[pl.BlockSpec((tm, MLA_KV_LORA), row), pl.BlockSpec((tm, LANE), row),
                  pl.BlockSpec(wk.shape, fix), pl.BlockSpec(wv.shape, fix)],
        out_specs=[pl.BlockSpec((tm, MLA_HEADS * AUG), row), pl.BlockSpec((tm, MLA_HEADS * MLA_V), row)],
        out_shape=[jax.ShapeDtypeStruct((r, MLA_HEADS * AUG), bf16),
                   jax.ShapeDtypeStruct((r, MLA_HEADS * MLA_V), bf16)],
        compiler_params=_cparams(("arbitrary",)),
        name="mla_kv",
    )(ckvb, krb, wk, wv)


def _qlat_kernel(q_ref, w_ref, o_ref):
    o_ref[...] = _dot(q_ref[...], w_ref[0])


def _mla_qlat(qfull, wukt):
    r = qfull.shape[0]
    return pl.pallas_call(
        _qlat_kernel,
        grid=(MLA_HEADS,),
        in_specs=[pl.BlockSpec((r, LANE), lambda h: (0, 2 * h)),
                  pl.BlockSpec((1, MLA_NOPE, MLA_KV_LORA), lambda h: (h, 0, 0))],
        out_specs=pl.BlockSpec((r, MLA_KV_LORA), lambda h: (0, h)),
        out_shape=jax.ShapeDtypeStruct((r, MLA_HEADS * MLA_KV_LORA), f32),
        compiler_params=_cparams(("arbitrary",)),
        name="mla_qlat",
    )(qfull, wukt)


def _uv_kernel(o_ref, w_ref, y_ref):
    y_ref[...] = _dot(o_ref[0], w_ref[0]).astype(y_ref.dtype)


def _mla_uv(olat_t, wuvh):
    r = olat_t.shape[1]
    return pl.pallas_call(
        _uv_kernel,
        grid=(MLA_HEADS,),
        in_specs=[pl.BlockSpec((1, r, MLA_KV_LORA), lambda h: (h, 0, 0)),
                  pl.BlockSpec((1, MLA_KV_LORA, MLA_V), lambda h: (h, 0, 0))],
        out_specs=pl.BlockSpec((r, MLA_V), lambda h: (0, h)),
        out_shape=jax.ShapeDtypeStruct((r, MLA_HEADS * MLA_V), bf16),
        compiler_params=_cparams(("arbitrary",)),
        name="mla_uv",
    )(olat_t, wuvh)


def _cumsum_kernel(l_ref, init_ref, c_ref):
    r = lax.broadcasted_iota(jnp.int32, (LANE, LANE), 0)
    c = lax.broadcasted_iota(jnp.int32, (LANE, LANE), 1)
    tri = jnp.where(c <= r, 1.0, 0.0).astype(bf16)
    carry = init_ref[...]
    for j in range(l_ref.shape[1] // LANE):
        rows = slice(j * LANE, (j + 1) * LANE)
        hi, mid, lo = _split3(l_ref[0, rows, :])
        within = _dot(tri, hi) + _dot(tri, mid) + _dot(tri, lo)
        c_ref[0, rows, :] = within + carry
        carry = carry + within[LANE - 1:LANE, :]


def _cumsum(lf, init):
    nb, t, _ = lf.shape
    return pl.pallas_call(
        _cumsum_kernel,
        grid=(nb,),
        in_specs=[pl.BlockSpec((1, t, LANE), lambda b: (b, 0, 0)),
                  pl.BlockSpec((1, LANE), lambda b: (0, 0))],
        out_specs=pl.BlockSpec((1, t, LANE), lambda b: (b, 0, 0)),
        out_shape=jax.ShapeDtypeStruct((nb, t, LANE), f32),
        compiler_params=_cparams(("arbitrary",)),
        name="logf_cumsum",
    )(lf, init)


def _fox_pack_kernel(q_ref, k_ref, c_ref, pq_ref, pk_ref, oq_ref, ok_ref, qa_ref, ka_ref):
    hi, mid, lo = _split3(c_ref[...])
    eq = _dot(hi, pq_ref[0]) + _dot(mid, pq_ref[1]) + _dot(lo, pq_ref[2]) + oq_ref[...]
    ek = _dot(hi, pk_ref[0]) + _dot(mid, pk_ref[1]) + _dot(lo, pk_ref[2]) + ok_ref[...]
    for h in range(FOX_HEADS):
        qa_ref[:, AUG * h:AUG * h + LANE] = q_ref[:, LANE * h:LANE * (h + 1)]
        qa_ref[:, AUG * h + LANE:AUG * (h + 1)] = eq[:, LANE * h:LANE * (h + 1)].astype(bf16)
    for g in range(FOX_KV_HEADS):
        ka_ref[:, AUG * g:AUG * g + LANE] = k_ref[:, LANE * g:LANE * (g + 1)].astype(bf16)
        ka_ref[:, AUG * g + LANE:AUG * (g + 1)] = ek[:, LANE * g:LANE * (g + 1)].astype(bf16)


def _fox_pack(q, k, c, tm):
    r = q.shape[0]
    pq = np.zeros((3, LANE, FOX_HEADS * LANE), np.float32)
    oq = np.zeros((1, FOX_HEADS * LANE), np.float32)
    pk = np.zeros((3, LANE, FOX_KV_HEADS * LANE), np.float32)
    ok = np.zeros((1, FOX_KV_HEADS * LANE), np.float32)
    for h in range(FOX_HEADS):
        g, rr = divmod(h, FOX_GROUP)
        for x in range(3):
            pq[x, h, h * LANE + x] = 1.0
            oq[0, h * LANE + 3 + 3 * rr + x] = 1.0
            pk[x, h, g * LANE + 3 + 3 * rr + x] = -1.0
    for g in range(FOX_KV_HEADS):
        ok[0, g * LANE:g * LANE + 3] = 1.0
    row = lambda i: (i, 0)
    fix2 = lambda i: (0, 0)
    fix3 = lambda i: (0, 0, 0)
    return pl.pallas_call(
        _fox_pack_kernel,
        grid=(r // tm,),
        in_specs=[pl.BlockSpec((tm, FOX_Q_W), row), pl.BlockSpec((tm, FOX_KV_W), row),
                  pl.BlockSpec((tm, LANE), row), pl.BlockSpec(pq.shape, fix3), pl.BlockSpec(pk.shape, fix3),
                  pl.BlockSpec(oq.shape, fix2), pl.BlockSpec(ok.shape, fix2)],
        out_specs=[pl.BlockSpec((tm, FOX_HEADS * AUG), row), pl.BlockSpec((tm, FOX_KV_HEADS * AUG), row)],
        out_shape=[jax.ShapeDtypeStruct((r, FOX_HEADS * AUG), bf16),
                   jax.ShapeDtypeStruct((r, FOX_KV_HEADS * AUG), bf16)],
        compiler_params=_cparams(("arbitrary",)),
        name="fox_pack",
    )(q, k, c, jnp.asarray(pq, bf16), jnp.asarray(pk, bf16), jnp.asarray(oq), jnp.asarray(ok))


def _flash_kernel(q_ref, k_ref, vt_ref, km_ref, vmt_ref, o_ref, *, tq, nh, shared_kv):
    qi = pl.program_id(2)
    qs = [q_ref[:, AUG * h:AUG * (h + 1)] for h in range(nh)]
    kcol = lambda h: slice(0, AUG) if shared_kv else slice(AUG * h, AUG * (h + 1))
    vrow = lambda h: slice(0, LANE) if shared_kv else slice(LANE * h, LANE * (h + 1))

    heads = range(nh)

    def scores(k_tile):
        return tuple(_dot_nt(k_tile[:, kcol(h)], qs[h]) for h in heads)

    def update(s, vt_tile, carries, mask):
        if mask is not None:
            s = [jnp.where(mask, s[h], NEG_INF) for h in heads]
        m_new = [jnp.maximum(carries[h][0], jnp.max(s[h], axis=0, keepdims=True)) for h in heads]
        alpha = [jnp.exp(carries[h][0] - m_new[h]) for h in heads]
        p = [jnp.exp(s[h] - m_new[h]) for h in heads]
        l = [alpha[h] * carries[h][1] + jnp.sum(p[h], axis=0, keepdims=True) for h in heads]
        pv = [_dot(vt_tile[vrow(h), :], p[h].astype(bf16)) for h in heads]
        return tuple((m_new[h], l[h], alpha[h] * carries[h][2] + pv[h]) for h in heads)

    def key_tile(j):
        return k_ref[pl.ds(pl.multiple_of(j * tq, tq), tq), :]

    init = (jnp.full((1, tq), NEG_INF, f32), jnp.zeros((1, tq), f32), jnp.zeros((LANE, tq), f32))
    meta_key = lax.broadcasted_iota(jnp.int32, (LANE, tq), 0)
    s_meta = scores(km_ref[...])
    s_cur = scores(key_tile(0))
    carries = update(s_meta, vmt_ref[...], (init,) * nh, meta_key < N_META)

    def body(j, state):
        s_cur, carries = state
        s_next = scores(key_tile(j + 1))
        return s_next, update(s_cur, vt_ref[j], carries, None)

    s_cur, carries = lax.fori_loop(0, qi, body, (s_cur, carries))
    key = lax.broadcasted_iota(jnp.int32, (tq, tq), 0)
    query = lax.broadcasted_iota(jnp.int32, (tq, tq), 1)
    carries = update(s_cur, vt_ref[qi], carries, key <= query)
    for h in range(nh):
        _, l, acc = carries[h]
        o_ref[:, LANE * h:LANE * (h + 1)] = (acc / l).T.astype(o_ref.dtype)


def _flash(q, k, vt, km, vmt, nb, seq, heads, nh, kv_group, tq):
    nq = seq // tq
    shared_kv = kv_group > 1
    assert kv_group % nh == 0 or not shared_kv
    kw = AUG if shared_kv else nh * AUG
    vw = LANE if shared_kv else nh * LANE
    kv = (lambda g: g * nh // kv_group) if shared_kv else (lambda g: g)
    return pl.pallas_call(
        functools.partial(_flash_kernel, tq=tq, nh=nh, shared_kv=shared_kv),
        grid=(nb, heads // nh, nq),
        in_specs=[pl.BlockSpec((tq, nh * AUG), lambda b, g, i: (b * nq + i, g)),
                  pl.BlockSpec((seq, kw), lambda b, g, i: (b, kv(g))),
                  pl.BlockSpec((nq, vw, tq), lambda b, g, i: (b, kv(g), 0)),
                  pl.BlockSpec((LANE, kw), lambda b, g, i: (0, kv(g))),
                  pl.BlockSpec((vw, LANE), lambda b, g, i: (kv(g), 0))],
        out_specs=pl.BlockSpec((tq, nh * LANE), lambda b, g, i: (b * nq + i, g)),
        out_shape=jax.ShapeDtypeStruct((nb * seq, heads * LANE), bf16),
        compiler_params=_cparams(("arbitrary", "arbitrary", "arbitrary")),
        name="prompt_attn",
    )(q, k, vt, km, vmt)


def _decode_kernel(pt_ref, qf_ref, cn_ref, kn_ref, vn_ref, ql_ref, qr_ref, cnew_ref, rnew_ref,
                   ck_hbm, cv_hbm, clf_hbm, cc_hbm, cr_hbm,
                   of_ref, ol_ref,
                   kbuf, vbuf, lfbuf, cbuf, rbuf, sems, mf, lfs, accf, mm, lms, accm, carry,
                   *, n_chunks, n_pages):
    b = pl.program_id(0)
    c = pl.program_id(1)
    nb = pl.num_programs(0)
    t = b * n_chunks + c
    slot = t % DEC_SLOTS
    t_last = nb * n_chunks - 1
    npg = DEC_PAGES
    kv_rows = PAGE_SIZE * FOX_KV_HEADS

    def copies(bb, cc, sl):
        out = []
        for p in range(npg):
            pid = pt_ref[bb, (n_chunks - 1 - cc) * npg + p]
            out.append(pltpu.make_async_copy(ck_hbm.at[pl.ds(pid * kv_rows, kv_rows), :],
                                             kbuf.at[sl, pl.ds(p * kv_rows, kv_rows), :], sems.at[sl, 0]))
            out.append(pltpu.make_async_copy(cv_hbm.at[pl.ds(pid * kv_rows, kv_rows), :],
                                             vbuf.at[sl, pl.ds(p * kv_rows, kv_rows), :], sems.at[sl, 1]))
            out.append(pltpu.make_async_copy(clf_hbm.at[pid],
                                             lfbuf.at[sl, pl.ds(p * FOX_HEADS, FOX_HEADS), :], sems.at[sl, 2]))
            out.append(pltpu.make_async_copy(cc_hbm.at[pid],
                                             cbuf.at[sl, pl.ds(p * PAGE_SIZE, PAGE_SIZE), :], sems.at[sl, 3]))
            out.append(pltpu.make_async_copy(cr_hbm.at[pid],
                                             rbuf.at[sl, :, pl.ds(p * PAGE_SIZE, PAGE_SIZE)], sems.at[sl, 4]))
        return out

    def start_chunk(tt):
        for cp in copies(tt // n_chunks, tt % n_chunks, tt % DEC_SLOTS):
            cp.start()

    @pl.when(t == 0)
    def _():
        for tt in range(DEC_SLOTS - 1):
            start_chunk(tt)

    @pl.when(t + DEC_SLOTS - 1 <= t_last)
    def _():
        start_chunk(t + DEC_SLOTS - 1)

    @pl.when(c == 0)
    def _():
        mf[...] = jnp.full(mf.shape, NEG_INF, f32)
        lfs[...] = jnp.zeros(lfs.shape, f32)
        accf[...] = jnp.zeros(accf.shape, f32)
        mm[...] = jnp.full(mm.shape, NEG_INF, f32)
        lms[...] = jnp.zeros(lms.shape, f32)
        accm[...] = jnp.zeros(accm.shape, f32)
        carry[...] = jnp.zeros(carry.shape, f32)

    for cp in copies(b, c, slot):
        cp.wait()

    qf = qf_ref[0]
    stacked = lfbuf[slot]
    r = lax.broadcasted_iota(jnp.int32, (PAGE_SIZE, kv_rows), 0)
    cidx = lax.broadcasted_iota(jnp.int32, (PAGE_SIZE, kv_rows), 1)
    tri = jnp.where(r > cidx // FOX_KV_HEADS, 1.0, 0.0).astype(bf16)
    pieces = [stacked[p * FOX_HEADS:(p + 1) * FOX_HEADS, :] for p in range(npg)]
    hi, mid, lo = _split3(stacked)
    within = _dot(hi, tri) + _dot(mid, tri) + _dot(lo, tri)
    run = carry[:, :1]
    bias = [None] * npg
    for p in reversed(range(npg)):
        bias[p] = within[p * FOX_HEADS:(p + 1) * FOX_HEADS, :] + run
        run = run + jnp.sum(pieces[p], axis=-1, keepdims=True)
    carry[...] = jnp.broadcast_to(run, carry.shape)
    def softmax_piece(s):
        m = jnp.max(s, axis=-1, keepdims=True)
        p = jnp.exp(s - m)
        return m, jnp.sum(p, axis=-1, keepdims=True), p

    def fold(parts, m_ref, l_ref, acc_ref):
        m_old = m_ref[:, :1]
        m_new = m_old
        for m, _, _ in parts:
            m_new = jnp.maximum(m_new, m)
        w_old = jnp.exp(m_old - m_new)
        l = w_old * l_ref[...]
        acc = w_old * acc_ref[...]
        for m, lp, ap in parts:
            w = jnp.exp(m - m_new)
            l = l + w * lp
            acc = acc + w * ap
        m_ref[...] = jnp.broadcast_to(m_new, m_ref.shape)
        l_ref[...] = l
        acc_ref[...] = acc

    pps = npg // DEC_SUB
    sub_kv = pps * kv_rows
    head = lax.broadcasted_iota(jnp.int32, (FOX_HEADS, sub_kv), 0)
    kvrow = lax.broadcasted_iota(jnp.int32, (FOX_HEADS, sub_kv), 1)
    own_kv = head // FOX_GROUP == kvrow % FOX_KV_HEADS
    cn = cn_ref[0][:, :1]
    sub_k = pps * PAGE_SIZE
    kv_blk = lambda i: pl.ds(i * sub_kv, sub_kv)
    k_blk = lambda i: pl.ds(i * sub_k, sub_k)
    subs = range(DEC_SUB)
    s_fox = [_dot_nt(qf, kbuf[slot, kv_blk(i), :]) for i in subs]
    s_mla = [_dot_nt(ql_ref[0], cbuf[slot, k_blk(i), :]) + _dot(qr_ref[0], rbuf[slot, :, k_blk(i)])
             for i in subs]
    p_fox = [softmax_piece(jnp.where(
        own_kv, s_fox[i] + jnp.concatenate(bias[i * pps:(i + 1) * pps], axis=1) + cn, NEG_INF)) for i in subs]
    p_mla = [softmax_piece(s_mla[i]) for i in subs]
    a_fox = [_dot(p_fox[i][2], vbuf[slot, kv_blk(i), :]) for i in subs]
    a_mla = [_dot(p_mla[i][2], cbuf[slot, k_blk(i), :]) for i in subs]
    fold([(p_fox[i][0], p_fox[i][1], a_fox[i]) for i in subs], mf, lfs, accf)
    fold([(p_mla[i][0], p_mla[i][1], a_mla[i]) for i in subs], mm, lms, accm)

    @pl.when(c == n_chunks - 1)
    def _():
        s_self = jnp.sum(qf * kn_ref[0], axis=-1, keepdims=True)
        m_old = mf[:, :1]
        m_new = jnp.maximum(m_old, s_self)
        alpha = jnp.exp(m_old - m_new)
        p_self = jnp.exp(s_self - m_new)
        l = alpha * lfs[:, :1] + p_self
        of_ref[0] = (alpha * accf[...] + p_self * vn_ref[0]) / l

        cnew = cnew_ref[0]
        s_self = (jnp.sum(ql_ref[0] * cnew, axis=-1, keepdims=True)
                  + jnp.sum(qr_ref[0] * rnew_ref[0], axis=-1, keepdims=True))
        m_old = mm[:, :1]
        m_new = jnp.maximum(m_old, s_self)
        alpha = jnp.exp(m_old - m_new)
        p_self = jnp.exp(s_self - m_new)
        l = alpha * lms[:, :1] + p_self
        ol_ref[0] = (alpha * accm[...] + p_self * cnew) / l


def _decode(page_table, qf, cn, kn, vn, ql, qr, cnew, rnew, ck, cv, clf, cc, cr):
    nb, n_pages = page_table.shape
    n_chunks = n_pages // DEC_PAGES
    kv_rows = PAGE_SIZE * FOX_KV_HEADS
    per_b = lambda shape: pl.BlockSpec((1,) + shape, lambda b, c, pt: (b, 0, 0))
    any_spec = pl.BlockSpec(memory_space=pl.ANY)
    grid_spec = pltpu.PrefetchScalarGridSpec(
        num_scalar_prefetch=1,
        grid=(nb, n_chunks),
        in_specs=[per_b((FOX_HEADS, FOX_HEAD_DIM)), per_b((FOX_HEADS, LANE)),
                  per_b((FOX_HEADS, FOX_HEAD_DIM)), per_b((FOX_HEADS, FOX_HEAD_DIM)),
                  per_b((MLA_HEADS, MLA_KV_LORA)), per_b((MLA_HEADS, MLA_ROPE)),
                  per_b((1, MLA_KV_LORA)), per_b((1, MLA_ROPE)),
                  any_spec, any_spec, any_spec, any_spec, any_spec],
        out_specs=[per_b((FOX_HEADS, FOX_HEAD_DIM)), per_b((MLA_HEADS, MLA_KV_LORA))],
        scratch_shapes=[
            pltpu.VMEM((DEC_SLOTS, DEC_PAGES * kv_rows, FOX_HEAD_DIM), f32),
            pltpu.VMEM((DEC_SLOTS, DEC_PAGES * kv_rows, FOX_HEAD_DIM), f32),
            pltpu.VMEM((DEC_SLOTS, DEC_PAGES * FOX_HEADS, PAGE_SIZE), f32),
            pltpu.VMEM((DEC_SLOTS, DEC_PAGES * PAGE_SIZE, MLA_KV_LORA), f32),
            pltpu.VMEM((DEC_SLOTS, MLA_ROPE, DEC_PAGES * PAGE_SIZE), f32),
            pltpu.SemaphoreType.DMA((DEC_SLOTS, 5)),
            pltpu.VMEM((FOX_HEADS, LANE), f32), pltpu.VMEM((FOX_HEADS, LANE), f32),
            pltpu.VMEM((FOX_HEADS, FOX_HEAD_DIM), f32),
            pltpu.VMEM((MLA_HEADS, LANE), f32), pltpu.VMEM((MLA_HEADS, LANE), f32),
            pltpu.VMEM((MLA_HEADS, MLA_KV_LORA), f32),
            pltpu.VMEM((FOX_HEADS, LANE), f32),
        ],
    )
    return pl.pallas_call(
        functools.partial(_decode_kernel, n_chunks=n_chunks, n_pages=n_pages),
        grid_spec=grid_spec,
        out_shape=[jax.ShapeDtypeStruct((nb, FOX_HEADS, FOX_HEAD_DIM), f32),
                   jax.ShapeDtypeStruct((nb, MLA_HEADS, MLA_KV_LORA), f32)],
        compiler_params=_cparams(("arbitrary", "arbitrary")),
        name="paged_decode_attn",
    )(page_table, qf, cn, kn, vn, ql, qr, cnew, rnew, ck, cv, clf, cc, cr)


def _attn_tail_kernel(x_ref, g_ref, yf_ref, ym_ref, wga_ref, wgb_ref, wf_ref, wm_ref, wo_ref, o_ref,
                      hn_ref, xs_ref, mix_ref, *, nj, tn):
    j = pl.program_id(1)

    @pl.when(j == 0)
    def _():
        x = x_ref[...]
        hn_ref[...] = _rms(x, g_ref[...]).astype(bf16)
        for c in range(nj):
            xs_ref[c] = x[:, c * tn:(c + 1) * tn]

    @pl.when(j < nj)
    def _():
        hn = hn_ref[...]
        ga = 1.0 / (1.0 + jnp.exp(-_dot(hn, wga_ref[...])))
        gb = 1.0 / (1.0 + jnp.exp(-_dot(hn, wgb_ref[...])))
        mix = ga * _dot(yf_ref[...], wf_ref[...]) + gb * _dot(ym_ref[...], wm_ref[...])
        mix_ref[j] = mix.astype(bf16)

    @pl.when(j >= nj)
    def _():
        acc = xs_ref[j - nj]
        for c in range(nj):
            acc = acc + _dot(mix_ref[c], wo_ref[c * tn:(c + 1) * tn, :])
        o_ref[...] = acc


def _attn_tail(x, g, yf, ym, wg, wf, wm, wo, tm, tn):
    r = x.shape[0]
    nj = D_MODEL // tn
    mix_col = lambda i, j: (0, jnp.minimum(j, nj - 1))
    out_col = lambda i, j: (0, jnp.maximum(j - nj, 0))
    return pl.pallas_call(
        functools.partial(_attn_tail_kernel, nj=nj, tn=tn),
        grid=(r // tm, 2 * nj),
        in_specs=[pl.BlockSpec((tm, D_MODEL), lambda i, j: (i, 0)),
                  pl.BlockSpec((1, D_MODEL), lambda i, j: (0, 0)),
                  pl.BlockSpec((tm, FOX_Q_W), lambda i, j: (i, 0)),
                  pl.BlockSpec((tm, MLA_HEADS * MLA_V), lambda i, j: (i, 0)),
                  pl.BlockSpec((D_MODEL, tn), mix_col),
                  pl.BlockSpec((D_MODEL, tn), lambda i, j: (0, nj + jnp.minimum(j, nj - 1))),
                  pl.BlockSpec((FOX_Q_W, tn), mix_col),
                  pl.BlockSpec((MLA_HEADS * MLA_V, tn), mix_col),
                  pl.BlockSpec((D_MODEL, tn), out_col)],
        out_specs=pl.BlockSpec((tm, tn), lambda i, j: (i, jnp.maximum(j - nj, 0))),
        out_shape=jax.ShapeDtypeStruct((r, D_MODEL), f32),
        scratch_shapes=[pltpu.VMEM((tm, D_MODEL), bf16), pltpu.VMEM((nj, tm, tn), f32),
                        pltpu.VMEM((nj, tm, tn), bf16)],
        compiler_params=_cparams(("arbitrary", "arbitrary")),
        name="attn_tail",
    )(x, g, yf, ym, wg, wg, wf, wm, wo)


def _ffn_kernel(h_ref, g_ref, wg_ref, wu_ref, wd_ref, gf_ref, y_ref, hn_ref, acc_ref):
    j = pl.program_id(1)

    @pl.when(j == 0)
    def _():
        h = h_ref[...]
        hn_ref[...] = _rms(h, g_ref[...]).astype(bf16)
        acc_ref[...] = h

    hn = hn_ref[...]
    a = _dot(hn, wg_ref[...])
    u = _dot(hn, wu_ref[...])
    act = (a / (1.0 + jnp.exp(-a)) * u).astype(bf16)
    acc_ref[...] += _dot(act, wd_ref[...])

    @pl.when(j == pl.num_programs(1) - 1)
    def _():
        y_ref[...] = _rms(acc_ref[...], gf_ref[...])


def _ffn(h, g, wg, wu, wd, gf, tm, tf):
    r = h.shape[0]
    nf = wg.shape[1] // tf
    return pl.pallas_call(
        _ffn_kernel,
        grid=(r // tm, nf),
        in_specs=[pl.BlockSpec((tm, D_MODEL), lambda i, j: (i, 0)),
                  pl.BlockSpec((1, D_MODEL), lambda i, j: (0, 0)),
                  pl.BlockSpec((D_MODEL, tf), lambda i, j: (0, j)),
                  pl.BlockSpec((D_MODEL, tf), lambda i, j: (0, j)),
                  pl.BlockSpec((tf, D_MODEL), lambda i, j: (j, 0)),
                  pl.BlockSpec((1, D_MODEL), lambda i, j: (0, 0))],
        out_specs=pl.BlockSpec((tm, D_MODEL), lambda i, j: (i, 0)),
        out_shape=jax.ShapeDtypeStruct((r, D_MODEL), f32),
        scratch_shapes=[pltpu.VMEM((tm, D_MODEL), bf16), pltpu.VMEM((tm, D_MODEL), f32)],
        compiler_params=_cparams(("arbitrary", "arbitrary")),
        name="ffn",
    )(h, g, wg, wu, wd, gf)


def _rope_tables(pos):
    inv_freq = ROPE_BASE ** (-jnp.arange(0, MLA_ROPE, 2, dtype=f32) / MLA_ROPE)
    ang = pos.astype(f32)[:, None] * inv_freq[None, :]
    c, s = jnp.cos(ang), jnp.sin(ang)
    z = jnp.zeros((pos.shape[0], LANE - MLA_ROPE), f32)
    return jnp.concatenate([c, c, z], axis=1), jnp.concatenate([-s, s, z], axis=1)


def _swap_halves(w):
    half = w.shape[-1] // 2
    return jnp.concatenate([w[..., half:], w[..., :half]], axis=-1)


def kernel(x_prompt, x_sample, cache_fox_k, cache_fox_v, cache_fox_logf, cache_mla_ckv, cache_mla_krope,
           page_table, meta_tokens, attn_norm, w_in, fox_forget_bias, mla_q_norm, mla_kv_norm, mla_w_uq,
           mla_w_uk, mla_w_uv, w_branch_fox, w_branch_mla, w_out, ffn_norm, w_gate, w_up, w_down, final_norm):
    nb, seq, _ = x_prompt.shape
    db = x_sample.shape[0]
    n_pool = cache_fox_k.shape[1]
    past_len = page_table.shape[1] * PAGE_SIZE
    layer = 0

    w0 = w_in[layer]
    o_f = FOX_Q_W + 2 * FOX_KV_W
    o_cq = o_f + FOX_HEADS
    o_ckv = o_cq + MLA_Q_LORA
    o_kr = o_ckv + MLA_KV_LORA
    o_g = o_kr + MLA_ROPE
    w_qkv = w0[:, :o_f].astype(bf16)
    w_kr = w0[:, o_kr:o_g]
    zpad = lambda n: jnp.zeros((D_MODEL, n), f32)
    w_lat = jnp.concatenate([w0[:, o_cq:o_kr], w_kr, zpad(LANE - MLA_ROPE), _swap_halves(w_kr),
                             zpad(LANE - MLA_ROPE), w0[:, o_f:o_cq], zpad(LANE - FOX_HEADS)], axis=1).astype(bf16)
    w_g = w0[:, o_g:].astype(bf16)
    bias_f = jnp.concatenate([fox_forget_bias[layer], jnp.zeros((LANE - FOX_HEADS,), f32)])[None]
    wuq = mla_w_uq[layer].reshape(MLA_Q_LORA, MLA_HEADS, MLA_NOPE + MLA_ROPE)
    zq = jnp.zeros((MLA_Q_LORA, MLA_HEADS, LANE - MLA_ROPE), f32)
    wuq = jnp.concatenate([wuq[..., :MLA_NOPE], wuq[..., MLA_NOPE:], zq,
                           _swap_halves(wuq[..., MLA_NOPE:]), zq], axis=-1)
    wuq = wuq.reshape(MLA_Q_LORA, MLA_HEADS * 3 * LANE).astype(bf16)
    wuk = mla_w_uk[layer].reshape(MLA_KV_LORA, MLA_HEADS * MLA_NOPE).astype(bf16)
    wuv = mla_w_uv[layer].reshape(MLA_KV_LORA, MLA_HEADS * MLA_V).astype(bf16)
    wuk_t = mla_w_uk[layer].transpose(1, 2, 0)
    wuv_h = mla_w_uv[layer].transpose(1, 0, 2).astype(bf16)
    w_bf = w_branch_fox[layer].astype(bf16)
    w_bm = w_branch_mla[layer].astype(bf16)
    w_o = w_out[layer].astype(bf16)
    w_ga = w_gate[layer].astype(bf16)
    w_u = w_up[layer].astype(bf16)
    w_d = w_down[layer].astype(bf16)
    g_attn = attn_norm[layer][None]
    g_ffn = ffn_norm[layer][None]
    g_fin = final_norm[None]
    qn = mla_q_norm[layer][None]
    kvn = mla_kv_norm[layer][None]

    x_main = x_prompt.reshape(nb * seq, D_MODEL)
    x_small = jnp.concatenate([x_sample.reshape(db, D_MODEL), meta_tokens.astype(f32),
                               jnp.zeros((SMALL_ROWS - db - N_META, D_MODEL), f32)], axis=0)
    cos_m, sin_m = _rope_tables(N_META + jnp.arange(seq))
    pos_s = jnp.concatenate([jnp.full((db,), past_len, jnp.int32), jnp.arange(N_META, dtype=jnp.int32),
                             jnp.zeros((SMALL_ROWS - db - N_META,), jnp.int32)])
    cos_s, sin_s = _rope_tables(pos_s)

    def project(x, cos, sin, tm):
        q, k, v, vb = _qkv_proj(x, g_attn, w_qkv, tm)
        cq, ckv, ckvb, kr, krb, lf, lfp = _lat_proj(x, g_attn, w_lat, qn, kvn, bias_f, cos, sin, tm)
        return dict(q=q, k=k, v=v, vb=vb, cq=cq, ckv=ckv, ckvb=ckvb, kr=kr, krb=krb, lf=lf, lfp=lfp)

    pm = project(x_main, cos_m, sin_m, MAIN_TM)
    ps = project(x_small, cos_s, sin_s, SMALL_ROWS)

    meta_rows = slice(db, db + LANE)
    is_meta = (jnp.arange(LANE) < N_META)[:, None]
    lf_meta = jnp.where(is_meta, ps["lfp"][meta_rows], 0.0)[None]
    c_meta = _cumsum(lf_meta, jnp.zeros((1, LANE), f32))
    c_main = _cumsum(pm["lfp"].reshape(nb, seq, LANE), c_meta[0, N_META - 1:N_META])
    c_small = jnp.concatenate([jnp.zeros((db, LANE), f32), c_meta[0]], axis=0)

    qa_m, ka_m = _fox_pack(pm["q"], pm["k"], c_main.reshape(nb * seq, LANE), MAIN_TM)
    _, ka_s = _fox_pack(ps["q"], ps["k"], c_small, SMALL_ROWS)
    zero_meta = lambda a: jnp.where(is_meta, a[meta_rows], jnp.zeros((), a.dtype))
    def key_blocks_t(v):
        return v.reshape(nb * seq // ATTN_TQ, ATTN_TQ, v.shape[1]).transpose(0, 2, 1)

    y_fox = _flash(qa_m, ka_m, key_blocks_t(pm["vb"]), zero_meta(ka_s), zero_meta(ps["vb"]).T,
                   nb, seq, FOX_HEADS, ATTN_HEADS_PER_STEP, FOX_GROUP, ATTN_TQ)
    qm_m = _mla_q(pm["cq"], wuq, cos_m, sin_m, MAIN_TM, bf16)
    km_m, vm_m = _mla_kv(pm["ckvb"], pm["krb"], wuk, wuv, MAIN_TM)
    km_s, vm_s = _mla_kv(ps["ckvb"], ps["krb"], wuk, wuv, SMALL_ROWS)
    y_mla = _flash(qm_m, km_m, key_blocks_t(vm_m), zero_meta(km_s), zero_meta(vm_s).T,
                   nb, seq, MLA_HEADS, ATTN_HEADS_PER_STEP, 1, ATTN_TQ)

    qm_s = _mla_q(ps["cq"], wuq, cos_s, sin_s, SMALL_ROWS, f32)
    q_lat = _mla_qlat(qm_s, wuk_t)[:db].reshape(db, MLA_HEADS, MLA_KV_LORA)
    q_rope = qm_s[:db].reshape(db, MLA_HEADS, AUG)[:, :, LANE:LANE + MLA_ROPE]
    qf = ps["q"][:db].astype(f32).reshape(db, FOX_HEADS, FOX_HEAD_DIM)
    cn = jnp.broadcast_to(ps["lf"][:db, :, None], (db, FOX_HEADS, LANE))
    kn = jnp.repeat(ps["k"][:db].reshape(db, FOX_KV_HEADS, FOX_HEAD_DIM), FOX_GROUP, axis=1)
    vn = jnp.repeat(ps["v"][:db].reshape(db, FOX_KV_HEADS, FOX_HEAD_DIM), FOX_GROUP, axis=1)
    o_fox, o_lat = _decode(
        page_table, qf, cn, kn, vn, q_lat, q_rope,
        ps["ckv"][:db].reshape(db, 1, MLA_KV_LORA), ps["kr"][:db].reshape(db, 1, MLA_ROPE),
        cache_fox_k[layer].reshape(n_pool * PAGE_SIZE * FOX_KV_HEADS, FOX_HEAD_DIM),
        cache_fox_v[layer].reshape(n_pool * PAGE_SIZE * FOX_KV_HEADS, FOX_HEAD_DIM),
        jnp.swapaxes(cache_fox_logf[layer], 1, 2), cache_mla_ckv[layer],
        jnp.swapaxes(cache_mla_krope[layer], 1, 2))
    pad_rows = lambda a: jnp.concatenate([a, jnp.zeros((SMALL_ROWS - db,) + a.shape[1:], a.dtype)], axis=0)
    y_fox_s = pad_rows(o_fox.reshape(db, FOX_Q_W).astype(bf16))
    y_mla_s = _mla_uv(pad_rows(o_lat).transpose(1, 0, 2).astype(bf16), wuv_h)

    def tail(x, yf, ym, tm):
        h1 = _attn_tail(x, g_attn, yf, ym, w_g, w_bf, w_bm, w_o, tm, DENSE_TN)
        return _ffn(h1, g_ffn, w_ga, w_u, w_d, g_fin, tm, DENSE_TN)

    y_main = tail(x_main, y_fox, y_mla, MAIN_TM)
    y_small = tail(x_small, y_fox_s, y_mla_s, SMALL_ROWS)

    def prompt_out(name, tail_shape):
        meta = jnp.broadcast_to(ps[name][db:db + N_META][None], (nb, N_META) + ps[name].shape[1:])
        full = jnp.concatenate([meta, pm[name].reshape((nb, seq) + pm[name].shape[1:])], axis=1)
        return full.reshape((1, nb, seq + N_META) + tail_shape)

    def sample_out(name, tail_shape):
        return ps[name][:db].reshape((1, db, 1) + tail_shape)

    kv_shape = (FOX_KV_HEADS, FOX_HEAD_DIM)
    return (y_main.reshape(nb, seq, D_MODEL), y_small[:db].reshape(db, 1, D_MODEL),
            prompt_out("k", kv_shape), prompt_out("v", kv_shape), prompt_out("lf", (FOX_HEADS,)),
            prompt_out("ckv", (MLA_KV_LORA,)), prompt_out("kr", (MLA_ROPE,)),
            sample_out("k", kv_shape), sample_out("v", kv_shape), sample_out("lf", (FOX_HEADS,)),
            sample_out("ckv", (MLA_KV_LORA,)), sample_out("kr", (MLA_ROPE,)))
```

```python
import functools

import numpy as np
import jax
import jax.numpy as jnp
from jax import lax
from jax.experimental import pallas as pl
from jax.experimental.pallas import tpu as pltpu

D_MODEL = 2048
N_META = 16
RMS_EPS = 1e-6
NEG_INF = -1e30
FOX_HEADS = 8
FOX_KV_HEADS = 2
FOX_GROUP = FOX_HEADS // FOX_KV_HEADS
FOX_HEAD_DIM = 128
FOX_Q_W = FOX_HEADS * FOX_HEAD_DIM
FOX_KV_W = FOX_KV_HEADS * FOX_HEAD_DIM
FOX_SCALE = FOX_HEAD_DIM ** -0.5
MLA_HEADS = 8
MLA_Q_LORA = 512
MLA_KV_LORA = 512
MLA_NOPE = 128
MLA_ROPE = 64
MLA_V = 128
MLA_SCALE = (MLA_NOPE + MLA_ROPE) ** -0.5
ROPE_BASE = 10000.0
PAGE_SIZE = 128
LANE = 128
SMALL_ROWS = 256
AUG = 2 * LANE
DEC_PAGES = 16
DEC_SUB = 4
DEC_SLOTS = 3
MAIN_TM = 512
DENSE_TN = 512
ATTN_TQ = 512
ATTN_HEADS_PER_STEP = 2
VMEM_LIMIT = 56 * 1024 * 1024

bf16 = jnp.bfloat16
f32 = jnp.float32


def _cparams(sem):
    return pltpu.CompilerParams(dimension_semantics=sem, vmem_limit_bytes=VMEM_LIMIT)


def _rms(x, g):
    return x * lax.rsqrt(jnp.mean(x * x, axis=-1, keepdims=True) + RMS_EPS) * g


def _dot(a, b):
    return jnp.dot(a, b, preferred_element_type=f32)


def _dot_nt(a, b):
    return lax.dot_general(a, b, (((1,), (1,)), ((), ())), preferred_element_type=f32)


def _split3(c):
    hi = c.astype(bf16)
    r1 = c - hi.astype(f32)
    mid = r1.astype(bf16)
    lo = (r1 - mid.astype(f32)).astype(bf16)
    return hi, mid, lo


def _qkv_kernel(x_ref, g_ref, w_ref, q_ref, k_ref, v_ref, vb_ref):
    hn = _rms(x_ref[...], g_ref[...]).astype(bf16)
    z = _dot(hn, w_ref[...])
    q_ref[...] = (z[:, :FOX_Q_W] * FOX_SCALE).astype(bf16)
    k_ref[...] = z[:, FOX_Q_W:FOX_Q_W + FOX_KV_W]
    v = z[:, FOX_Q_W + FOX_KV_W:]
    v_ref[...] = v
    vb_ref[...] = v.astype(bf16)


def _qkv_proj(x, g, w, tm):
    r = x.shape[0]
    n = w.shape[1]
    row = lambda i: (i, 0)
    fix = lambda i: (0, 0)
    return pl.pallas_call(
        _qkv_kernel,
        grid=(r // tm,),
        in_specs=[pl.BlockSpec((tm, D_MODEL), row), pl.BlockSpec((1, D_MODEL), fix),
                  pl.BlockSpec((D_MODEL, n), fix)],
        out_specs=[pl.BlockSpec((tm, FOX_Q_W), row), pl.BlockSpec((tm, FOX_KV_W), row),
                   pl.BlockSpec((tm, FOX_KV_W), row), pl.BlockSpec((tm, FOX_KV_W), row)],
        out_shape=[jax.ShapeDtypeStruct((r, FOX_Q_W), bf16), jax.ShapeDtypeStruct((r, FOX_KV_W), f32),
                   jax.ShapeDtypeStruct((r, FOX_KV_W), f32), jax.ShapeDtypeStruct((r, FOX_KV_W), bf16)],
        compiler_params=_cparams(("arbitrary",)),
        name="qkv_proj",
    )(x, g, w)


def _lat_kernel(x_ref, g_ref, w_ref, qn_ref, kvn_ref, bf_ref, cos_ref, sin_ref,
                cq_ref, ckv_ref, ckvb_ref, kr_ref, krb_ref, lf_ref, lfp_ref):
    hn = _rms(x_ref[...], g_ref[...]).astype(bf16)
    z = _dot(hn, w_ref[...])
    o = 0
    cq_ref[...] = _rms(z[:, o:o + MLA_Q_LORA], qn_ref[...]).astype(bf16)
    o += MLA_Q_LORA
    ckv = _rms(z[:, o:o + MLA_KV_LORA], kvn_ref[...])
    ckv_ref[...] = ckv
    ckvb_ref[...] = ckv.astype(bf16)
    o += MLA_KV_LORA
    kr = z[:, o:o + LANE] * cos_ref[...] + z[:, o + LANE:o + 2 * LANE] * sin_ref[...]
    kr_ref[...] = kr[:, :MLA_ROPE]
    krb_ref[...] = kr.astype(bf16)
    o += 2 * LANE
    zf = z[:, o:o + LANE] + bf_ref[...]
    lf = jnp.minimum(zf, 0.0) - jnp.log(1.0 + jnp.exp(-jnp.abs(zf)))
    lane = lax.broadcasted_iota(jnp.int32, lf.shape, 1)
    lf = jnp.where(lane < FOX_HEADS, lf, 0.0)
    lf_ref[...] = lf[:, :FOX_HEADS]
    lfp_ref[...] = lf


def _lat_proj(x, g, w, qn, kvn, bfp, cos, sin, tm):
    r = x.shape[0]
    n = w.shape[1]
    ntab = cos.shape[0] // tm
    row = lambda i: (i, 0)
    fix = lambda i: (0, 0)
    tab = lambda i: (i % ntab, 0)
    return pl.pallas_call(
        _lat_kernel,
        grid=(r // tm,),
        in_specs=[pl.BlockSpec((tm, D_MODEL), row), pl.BlockSpec((1, D_MODEL), fix),
                  pl.BlockSpec((D_MODEL, n), fix), pl.BlockSpec((1, MLA_Q_LORA), fix),
                  pl.BlockSpec((1, MLA_KV_LORA), fix), pl.BlockSpec((1, LANE), fix),
                  pl.BlockSpec((tm, LANE), tab), pl.BlockSpec((tm, LANE), tab)],
        out_specs=[pl.BlockSpec((tm, MLA_Q_LORA), row), pl.BlockSpec((tm, MLA_KV_LORA), row),
                   pl.BlockSpec((tm, MLA_KV_LORA), row), pl.BlockSpec((tm, MLA_ROPE), row),
                   pl.BlockSpec((tm, LANE), row), pl.BlockSpec((tm, FOX_HEADS), row),
                   pl.BlockSpec((tm, LANE), row)],
        out_shape=[jax.ShapeDtypeStruct((r, MLA_Q_LORA), bf16), jax.ShapeDtypeStruct((r, MLA_KV_LORA), f32),
                   jax.ShapeDtypeStruct((r, MLA_KV_LORA), bf16), jax.ShapeDtypeStruct((r, MLA_ROPE), f32),
                   jax.ShapeDtypeStruct((r, LANE), bf16), jax.ShapeDtypeStruct((r, FOX_HEADS), f32),
                   jax.ShapeDtypeStruct((r, LANE), f32)],
        compiler_params=_cparams(("arbitrary",)),
        name="lat_proj",
    )(x, g, w, qn, kvn, bfp, cos, sin)


def _mla_q_kernel(cq_ref, w_ref, cos_ref, sin_ref, q_ref):
    z = _dot(cq_ref[...], w_ref[...])
    cos = cos_ref[...]
    sin = sin_ref[...]
    for h in range(MLA_HEADS):
        o = 3 * LANE * h
        q_ref[:, AUG * h:AUG * h + LANE] = (z[:, o:o + LANE] * MLA_SCALE).astype(q_ref.dtype)
        rope = z[:, o + LANE:o + 2 * LANE] * cos + z[:, o + 2 * LANE:o + 3 * LANE] * sin
        q_ref[:, AUG * h + LANE:AUG * (h + 1)] = (rope * MLA_SCALE).astype(q_ref.dtype)


def _mla_q(cq, w, cos, sin, tm, out_dtype):
    r = cq.shape[0]
    ntab = cos.shape[0] // tm
    row = lambda i: (i, 0)
    fix = lambda i: (0, 0)
    tab = lambda i: (i % ntab, 0)
    return pl.pallas_call(
        _mla_q_kernel,
        grid=(r // tm,),
        in_specs=[pl.BlockSpec((tm, MLA_Q_LORA), row), pl.BlockSpec(w.shape, fix),
                  pl.BlockSpec((tm, LANE), tab), pl.BlockSpec((tm, LANE), tab)],
        out_specs=pl.BlockSpec((tm, MLA_HEADS * AUG), row),
        out_shape=jax.ShapeDtypeStruct((r, MLA_HEADS * AUG), out_dtype),
        compiler_params=_cparams(("arbitrary",)),
        name="mla_q",
    )(cq, w, cos, sin)


def _mla_kv_kernel(ckv_ref, kr_ref, wk_ref, wv_ref, k_ref, v_ref):
    ckv = ckv_ref[...]
    kn = _dot(ckv, wk_ref[...])
    kr = kr_ref[...]
    for h in range(MLA_HEADS):
        k_ref[:, AUG * h:AUG * h + LANE] = kn[:, MLA_NOPE * h:MLA_NOPE * (h + 1)].astype(bf16)
        k_ref[:, AUG * h + LANE:AUG * (h + 1)] = kr
    v_ref[...] = _dot(ckv, wv_ref[...]).astype(bf16)


def _mla_kv(ckvb, krb, wk, wv, tm):
    r = ckvb.shape[0]
    row = lambda i: (i, 0)
    fix = lambda i: (0, 0)
    return pl.pallas_call(
        _mla_kv_kernel,
        grid=(r // tm,),
        in_specs=[pl.BlockSpec((tm, MLA_KV_LORA), row), pl.BlockSpec((tm, LANE), row),
                  pl.BlockSpec(wk.shape, fix), pl.BlockSpec(wv.shape, fix)],
        out_specs=[pl.BlockSpec((tm, MLA_HEADS * AUG), row), pl.BlockSpec((tm, MLA_HEADS * MLA_V), row)],
        out_shape=[jax.ShapeDtypeStruct((r, MLA_HEADS * AUG), bf16),
                   jax.ShapeDtypeStruct((r, MLA_HEADS * MLA_V), bf16)],
        compiler_params=_cparams(("arbitrary",)),
        name="mla_kv",
    )(ckvb, krb, wk, wv)


def _qlat_kernel(q_ref, w_ref, o_ref):
    o_ref[...] = _dot(q_ref[...], w_ref[0])


def _mla_qlat(qfull, wukt):
    r = qfull.shape[0]
    return pl.pallas_call(
        _qlat_kernel,
        grid=(MLA_HEADS,),
        in_specs=[pl.BlockSpec((r, LANE), lambda h: (0, 2 * h)),
                  pl.BlockSpec((1, MLA_NOPE, MLA_KV_LORA), lambda h: (h, 0, 0))],
        out_specs=pl.BlockSpec((r, MLA_KV_LORA), lambda h: (0, h)),
        out_shape=jax.ShapeDtypeStruct((r, MLA_HEADS * MLA_KV_LORA), f32),
        compiler_params=_cparams(("arbitrary",)),
        name="mla_qlat",
    )(qfull, wukt)


def _uv_kernel(o_ref, w_ref, y_ref):
    y_ref[...] = _dot(o_ref[0], w_ref[0]).astype(y_ref.dtype)


def _mla_uv(olat_t, wuvh):
    r = olat_t.shape[1]
    return pl.pallas_call(
        _uv_kernel,
        grid=(MLA_HEADS,),
        in_specs=[pl.BlockSpec((1, r, MLA_KV_LORA), lambda h: (h, 0, 0)),
                  pl.BlockSpec((1, MLA_KV_LORA, MLA_V), lambda h: (h, 0, 0))],
        out_specs=pl.BlockSpec((r, MLA_V), lambda h: (0, h)),
        out_shape=jax.ShapeDtypeStruct((r, MLA_HEADS * MLA_V), bf16),
        compiler_params=_cparams(("arbitrary",)),
        name="mla_uv",
    )(olat_t, wuvh)


def _cumsum_kernel(l_ref, init_ref, c_ref):
    r = lax.broadcasted_iota(jnp.int32, (LANE, LANE), 0)
    c = lax.broadcasted_iota(jnp.int32, (LANE, LANE), 1)
    tri = jnp.where(c <= r, 1.0, 0.0).astype(bf16)
    carry = init_ref[...]
    for j in range(l_ref.shape[1] // LANE):
        rows = slice(j * LANE, (j + 1) * LANE)
        hi, mid, lo = _split3(l_ref[0, rows, :])
        within = _dot(tri, hi) + _dot(tri, mid) + _dot(tri, lo)
        c_ref[0, rows, :] = within + carry
        carry = carry + within[LANE - 1:LANE, :]


def _cumsum(lf, init):
    nb, t, _ = lf.shape
    return pl.pallas_call(
        _cumsum_kernel,
        grid=(nb,),
        in_specs=[pl.BlockSpec((1, t, LANE), lambda b: (b, 0, 0)),
                  pl.BlockSpec((1, LANE), lambda b: (0, 0))],
        out_specs=pl.BlockSpec((1, t, LANE), lambda b: (b, 0, 0)),
        out_shape=jax.ShapeDtypeStruct((nb, t, LANE), f32),
        compiler_params=_cparams(("arbitrary",)),
        name="logf_cumsum",
    )(lf, init)


def _fox_pack_kernel(q_ref, k_ref, c_ref, pq_ref, pk_ref, oq_ref, ok_ref, qa_ref, ka_ref):
    hi, mid, lo = _split3(c_ref[...])
    eq = _dot(hi, pq_ref[0]) + _dot(mid, pq_ref[1]) + _dot(lo, pq_ref[2]) + oq_ref[...]
    ek = _dot(hi, pk_ref[0]) + _dot(mid, pk_ref[1]) + _dot(lo, pk_ref[2]) + ok_ref[...]
    for h in range(FOX_HEADS):
        qa_ref[:, AUG * h:AUG * h + LANE] = q_ref[:, LANE * h:LANE * (h + 1)]
        qa_ref[:, AUG * h + LANE:AUG * (h + 1)] = eq[:, LANE * h:LANE * (h + 1)].astype(bf16)
    for g in range(FOX_KV_HEADS):
        ka_ref[:, AUG * g:AUG * g + LANE] = k_ref[:, LANE * g:LANE * (g + 1)].astype(bf16)
        ka_ref[:, AUG * g + LANE:AUG * (g + 1)] = ek[:, LANE * g:LANE * (g + 1)].astype(bf16)


def _fox_pack(q, k, c, tm):
    r = q.shape[0]
    pq = np.zeros((3, LANE, FOX_HEADS * LANE), np.float32)
    oq = np.zeros((1, FOX_HEADS * LANE), np.float32)
    pk = np.zeros((3, LANE, FOX_KV_HEADS * LANE), np.float32)
    ok = np.zeros((1, FOX_KV_HEADS * LANE), np.float32)
    for h in range(FOX_HEADS):
        g, rr = divmod(h, FOX_GROUP)
        for x in range(3):
            pq[x, h, h * LANE + x] = 1.0
            oq[0, h * LANE + 3 + 3 * rr + x] = 1.0
            pk[x, h, g * LANE + 3 + 3 * rr + x] = -1.0
    for g in range(FOX_KV_HEADS):
        ok[0, g * LANE:g * LANE + 3] = 1.0
    row = lambda i: (i, 0)
    fix2 = lambda i: (0, 0)
    fix3 = lambda i: (0, 0, 0)
    return pl.pallas_call(
        _fox_pack_kernel,
        grid=(r // tm,),
        in_specs=[pl.BlockSpec((tm, FOX_Q_W), row), pl.BlockSpec((tm, FOX_KV_W), row),
                  pl.BlockSpec((tm, LANE), row), pl.BlockSpec(pq.shape, fix3), pl.BlockSpec(pk.shape, fix3),
                  pl.BlockSpec(oq.shape, fix2), pl.BlockSpec(ok.shape, fix2)],
        out_specs=[pl.BlockSpec((tm, FOX_HEADS * AUG), row), pl.BlockSpec((tm, FOX_KV_HEADS * AUG), row)],
        out_shape=[jax.ShapeDtypeStruct((r, FOX_HEADS * AUG), bf16),
                   jax.ShapeDtypeStruct((r, FOX_KV_HEADS * AUG), bf16)],
        compiler_params=_cparams(("arbitrary",)),
        name="fox_pack",
    )(q, k, c, jnp.asarray(pq, bf16), jnp.asarray(pk, bf16), jnp.asarray(oq), jnp.asarray(ok))


def _flash_kernel(q_ref, k_ref, vt_ref, km_ref, vmt_ref, o_ref, *, tq, nh, shared_kv):
    qi = pl.program_id(2)
    qs = [q_ref[:, AUG * h:AUG * (h + 1)] for h in range(nh)]
    kcol = lambda h: slice(0, AUG) if shared_kv else slice(AUG * h, AUG * (h + 1))
    vrow = lambda h: slice(0, LANE) if shared_kv else slice(LANE * h, LANE * (h + 1))

    heads = range(nh)

    def scores(k_tile):
        return tuple(_dot_nt(k_tile[:, kcol(h)], qs[h]) for h in heads)

    def update(s, vt_tile, carries, mask):
        if mask is not None:
            s = [jnp.where(mask, s[h], NEG_INF) for h in heads]
        m_new = [jnp.maximum(carries[h][0], jnp.max(s[h], axis=0, keepdims=True)) for h in heads]
        alpha = [jnp.exp(carries[h][0] - m_new[h]) for h in heads]
        p = [jnp.exp(s[h] - m_new[h]) for h in heads]
        l = [alpha[h] * carries[h][1] + jnp.sum(p[h], axis=0, keepdims=True) for h in heads]
        pv = [_dot(vt_tile[vrow(h), :], p[h].astype(bf16)) for h in heads]
        return tuple((m_new[h], l[h], alpha[h] * carries[h][2] + pv[h]) for h in heads)

    def key_tile(j):
        return k_ref[pl.ds(pl.multiple_of(j * tq, tq), tq), :]

    init = (jnp.full((1, tq), NEG_INF, f32), jnp.zeros((1, tq), f32), jnp.zeros((LANE, tq), f32))
    meta_key = lax.broadcasted_iota(jnp.int32, (LANE, tq), 0)
    s_meta = scores(km_ref[...])
    s_cur = scores(key_tile(0))
    carries = update(s_meta, vmt_ref[...], (init,) * nh, meta_key < N_META)

    def body(j, state):
        s_cur, carries = state
        s_next = scores(key_tile(j + 1))
        return s_next, update(s_cur, vt_ref[j], carries, None)

    s_cur, carries = lax.fori_loop(0, qi, body, (s_cur, carries))
    key = lax.broadcasted_iota(jnp.int32, (tq, tq), 0)
    query = lax.broadcasted_iota(jnp.int32, (tq, tq), 1)
    carries = update(s_cur, vt_ref[qi], carries, key <= query)
    for h in range(nh):
        _, l, acc = carries[h]
        o_ref[:, LANE * h:LANE * (h + 1)] = (acc / l).T.astype(o_ref.dtype)


def _flash(q, k, vt, km, vmt, nb, seq, heads, nh, kv_group, tq):
    nq = seq // tq
    shared_kv = kv_group > 1
    assert kv_group % nh == 0 or not shared_kv
    kw = AUG if shared_kv else nh * AUG
    vw = LANE if shared_kv else nh * LANE
    kv = (lambda g: g * nh // kv_group) if shared_kv else (lambda g: g)
    return pl.pallas_call(
        functools.partial(_flash_kernel, tq=tq, nh=nh, shared_kv=shared_kv),
        grid=(nb, heads // nh, nq),
        in_specs=[pl.BlockSpec((tq, nh * AUG), lambda b, g, i: (b * nq + i, g)),
                  pl.BlockSpec((seq, kw), lambda b, g, i: (b, kv(g))),
                  pl.BlockSpec((nq, vw, tq), lambda b, g, i: (b, kv(g), 0)),
                  pl.BlockSpec((LANE, kw), lambda b, g, i: (0, kv(g))),
                  pl.BlockSpec((vw, LANE), lambda b, g, i: (kv(g), 0))],
        out_specs=pl.BlockSpec((tq, nh * LANE), lambda b, g, i: (b * nq + i, g)),
        out_shape=jax.ShapeDtypeStruct((nb * seq, heads * LANE), bf16),
        compiler_params=_cparams(("arbitrary", "arbitrary", "arbitrary")),
        name="prompt_attn",
    )(q, k, vt, km, vmt)


def _decode_kernel(pt_ref, qf_ref, cn_ref, kn_ref, vn_ref, ql_ref, qr_ref, cnew_ref, rnew_ref,
                   ck_hbm, cv_hbm, clf_hbm, cc_hbm, cr_hbm,
                   of_ref, ol_ref,
                   kbuf, vbuf, lfbuf, cbuf, rbuf, sems, mf, lfs, accf, mm, lms, accm, carry,
                   *, n_chunks, n_pages):
    b = pl.program_id(0)
    c = pl.program_id(1)
    nb = pl.num_programs(0)
    t = b * n_chunks + c
    slot = t % DEC_SLOTS
    t_last = nb * n_chunks - 1
    npg = DEC_PAGES
    kv_rows = PAGE_SIZE * FOX_KV_HEADS

    def copies(bb, cc, sl):
        out = []
        for p in range(npg):
            pid = pt_ref[bb, (n_chunks - 1 - cc) * npg + p]
            out.append(pltpu.make_async_copy(ck_hbm.at[pl.ds(pid * kv_rows, kv_rows), :],
                                             kbuf.at[sl, pl.ds(p * kv_rows, kv_rows), :], sems.at[sl, 0]))
            out.append(pltpu.make_async_copy(cv_hbm.at[pl.ds(pid * kv_rows, kv_rows), :],
                                             vbuf.at[sl, pl.ds(p * kv_rows, kv_rows), :], sems.at[sl, 1]))
            out.append(pltpu.make_async_copy(clf_hbm.at[pid],
                                             lfbuf.at[sl, pl.ds(p * FOX_HEADS, FOX_HEADS), :], sems.at[sl, 2]))
            out.append(pltpu.make_async_copy(cc_hbm.at[pid],
                                             cbuf.at[sl, pl.ds(p * PAGE_SIZE, PAGE_SIZE), :], sems.at[sl, 3]))
            out.append(pltpu.make_async_copy(cr_hbm.at[pid],
                                             rbuf.at[sl, :, pl.ds(p * PAGE_SIZE, PAGE_SIZE)], sems.at[sl, 4]))
        return out

    def start_chunk(tt):
        for cp in copies(tt // n_chunks, tt % n_chunks, tt % DEC_SLOTS):
            cp.start()

    @pl.when(t == 0)
    def _():
        for tt in range(DEC_SLOTS - 1):
            start_chunk(tt)

    @pl.when(t + DEC_SLOTS - 1 <= t_last)
    def _():
        start_chunk(t + DEC_SLOTS - 1)

    @pl.when(c == 0)
    def _():
        mf[...] = jnp.full(mf.shape, NEG_INF, f32)
        lfs[...] = jnp.zeros(lfs.shape, f32)
        accf[...] = jnp.zeros(accf.shape, f32)
        mm[...] = jnp.full(mm.shape, NEG_INF, f32)
        lms[...] = jnp.zeros(lms.shape, f32)
        accm[...] = jnp.zeros(accm.shape, f32)
        carry[...] = jnp.zeros(carry.shape, f32)

    for cp in copies(b, c, slot):
        cp.wait()

    qf = qf_ref[0]
    stacked = lfbuf[slot]
    r = lax.broadcasted_iota(jnp.int32, (PAGE_SIZE, kv_rows), 0)
    cidx = lax.broadcasted_iota(jnp.int32, (PAGE_SIZE, kv_rows), 1)
    tri = jnp.where(r > cidx // FOX_KV_HEADS, 1.0, 0.0).astype(bf16)
    pieces = [stacked[p * FOX_HEADS:(p + 1) * FOX_HEADS, :] for p in range(npg)]
    hi, mid, lo = _split3(stacked)
    within = _dot(hi, tri) + _dot(mid, tri) + _dot(lo, tri)
    run = carry[:, :1]
    bias = [None] * npg
    for p in reversed(range(npg)):
        bias[p] = within[p * FOX_HEADS:(p + 1) * FOX_HEADS, :] + run
        run = run + jnp.sum(pieces[p], axis=-1, keepdims=True)
    carry[...] = jnp.broadcast_to(run, carry.shape)
    def softmax_piece(s):
        m = jnp.max(s, axis=-1, keepdims=True)
        p = jnp.exp(s - m)
        return m, jnp.sum(p, axis=-1, keepdims=True), p

    def fold(parts, m_ref, l_ref, acc_ref):
        m_old = m_ref[:, :1]
        m_new = m_old
        for m, _, _ in parts:
            m_new = jnp.maximum(m_new, m)
        w_old = jnp.exp(m_old - m_new)
        l = w_old * l_ref[...]
        acc = w_old * acc_ref[...]
        for m, lp, ap in parts:
            w = jnp.exp(m - m_new)
            l = l + w * lp
            acc = acc + w * ap
        m_ref[...] = jnp.broadcast_to(m_new, m_ref.shape)
        l_ref[...] = l
        acc_ref[...] = acc

    pps = npg // DEC_SUB
    sub_kv = pps * kv_rows
    head = lax.broadcasted_iota(jnp.int32, (FOX_HEADS, sub_kv), 0)
    kvrow = lax.broadcasted_iota(jnp.int32, (FOX_HEADS, sub_kv), 1)
    own_kv = head // FOX_GROUP == kvrow % FOX_KV_HEADS
    cn = cn_ref[0][:, :1]
    sub_k = pps * PAGE_SIZE
    kv_blk = lambda i: pl.ds(i * sub_kv, sub_kv)
    k_blk = lambda i: pl.ds(i * sub_k, sub_k)
    subs = range(DEC_SUB)
    s_fox = [_dot_nt(qf, kbuf[slot, kv_blk(i), :]) for i in subs]
    s_mla = [_dot_nt(ql_ref[0], cbuf[slot, k_blk(i), :]) + _dot(qr_ref[0], rbuf[slot, :, k_blk(i)])
             for i in subs]
    p_fox = [softmax_piece(jnp.where(
        own_kv, s_fox[i] + jnp.concatenate(bias[i * pps:(i + 1) * pps], axis=1) + cn, NEG_INF)) for i in subs]
    p_mla = [softmax_piece(s_mla[i]) for i in subs]
    a_fox = [_dot(p_fox[i][2], vbuf[slot, kv_blk(i), :]) for i in subs]
    a_mla = [_dot(p_mla[i][2], cbuf[slot, k_blk(i), :]) for i in subs]
    fold([(p_fox[i][0], p_fox[i][1], a_fox[i]) for i in subs], mf, lfs, accf)
    fold([(p_mla[i][0], p_mla[i][1], a_mla[i]) for i in subs], mm, lms, accm)

    @pl.when(c == n_chunks - 1)
    def _():
        s_self = jnp.sum(qf * kn_ref[0], axis=-1, keepdims=True)
        m_old = mf[:, :1]
        m_new = jnp.maximum(m_old, s_self)
        alpha = jnp.exp(m_old - m_new)
        p_self = jnp.exp(s_self - m_new)
        l = alpha * lfs[:, :1] + p_self
        of_ref[0] = (alpha * accf[...] + p_self * vn_ref[0]) / l

        cnew = cnew_ref[0]
        s_self = (jnp.sum(ql_ref[0] * cnew, axis=-1, keepdims=True)
                  + jnp.sum(qr_ref[0] * rnew_ref[0], axis=-1, keepdims=True))
        m_old = mm[:, :1]
        m_new = jnp.maximum(m_old, s_self)
        alpha = jnp.exp(m_old - m_new)
        p_self = jnp.exp(s_self - m_new)
        l = alpha * lms[:, :1] + p_self
        ol_ref[0] = (alpha * accm[...] + p_self * cnew) / l


def _decode(page_table, qf, cn, kn, vn, ql, qr, cnew, rnew, ck, cv, clf, cc, cr):
    nb, n_pages = page_table.shape
    n_chunks = n_pages // DEC_PAGES
    kv_rows = PAGE_SIZE * FOX_KV_HEADS
    per_b = lambda shape: pl.BlockSpec((1,) + shape, lambda b, c, pt: (b, 0, 0))
    any_spec = pl.BlockSpec(memory_space=pl.ANY)
    grid_spec = pltpu.PrefetchScalarGridSpec(
        num_scalar_prefetch=1,
        grid=(nb, n_chunks),
        in_specs=[per_b((FOX_HEADS, FOX_HEAD_DIM)), per_b((FOX_HEADS, LANE)),
                  per_b((FOX_HEADS, FOX_HEAD_DIM)), per_b((FOX_HEADS, FOX_HEAD_DIM)),
                  per_b((MLA_HEADS, MLA_KV_LORA)), per_b((MLA_HEADS, MLA_ROPE)),
                  per_b((1, MLA_KV_LORA)), per_b((1, MLA_ROPE)),
                  any_spec, any_spec, any_spec, any_spec, any_spec],
        out_specs=[per_b((FOX_HEADS, FOX_HEAD_DIM)), per_b((MLA_HEADS, MLA_KV_LORA))],
        scratch_shapes=[
            pltpu.VMEM((DEC_SLOTS, DEC_PAGES * kv_rows, FOX_HEAD_DIM), f32),
            pltpu.VMEM((DEC_SLOTS, DEC_PAGES * kv_rows, FOX_HEAD_DIM), f32),
            pltpu.VMEM((DEC_SLOTS, DEC_PAGES * FOX_HEADS, PAGE_SIZE), f32),
            pltpu.VMEM((DEC_SLOTS, DEC_PAGES * PAGE_SIZE, MLA_KV_LORA), f32),
            pltpu.VMEM((DEC_SLOTS, MLA_ROPE, DEC_PAGES * PAGE_SIZE), f32),
            pltpu.SemaphoreType.DMA((DEC_SLOTS, 5)),
            pltpu.VMEM((FOX_HEADS, LANE), f32), pltpu.VMEM((FOX_HEADS, LANE), f32),
            pltpu.VMEM((FOX_HEADS, FOX_HEAD_DIM), f32),
            pltpu.VMEM((MLA_HEADS, LANE), f32), pltpu.VMEM((MLA_HEADS, LANE), f32),
            pltpu.VMEM((MLA_HEADS, MLA_KV_LORA), f32),
            pltpu.VMEM((FOX_HEADS, LANE), f32),
        ],
    )
    return pl.pallas_call(
        functools.partial(_decode_kernel, n_chunks=n_chunks, n_pages=n_pages),
        grid_spec=grid_spec,
        out_shape=[jax.ShapeDtypeStruct((nb, FOX_HEADS, FOX_HEAD_DIM), f32),
                   jax.ShapeDtypeStruct((nb, MLA_HEADS, MLA_KV_LORA), f32)],
        compiler_params=_cparams(("arbitrary", "arbitrary")),
        name="paged_decode_attn",
    )(page_table, qf, cn, kn, vn, ql, qr, cnew, rnew, ck, cv, clf, cc, cr)


def _attn_tail_kernel(x_ref, g_ref, yf_ref, ym_ref, wga_ref, wgb_ref, wf_ref, wm_ref, wo_ref, o_ref,
                      hn_ref, xs_ref, mix_ref, *, nj, tn):
    j = pl.program_id(1)

    @pl.when(j == 0)
    def _():
        x = x_ref[...]
        hn_ref[...] = _rms(x, g_ref[...]).astype(bf16)
        for c in range(nj):
            xs_ref[c] = x[:, c * tn:(c + 1) * tn]

    @pl.when(j < nj)
    def _():
        hn = hn_ref[...]
        ga = 1.0 / (1.0 + jnp.exp(-_dot(hn, wga_ref[...])))
        gb = 1.0 / (1.0 + jnp.exp(-_dot(hn, wgb_ref[...])))
        mix = ga * _dot(yf_ref[...], wf_ref[...]) + gb * _dot(ym_ref[...], wm_ref[...])
        mix_ref[j] = mix.astype(bf16)

    @pl.when(j >= nj)
    def _():
        acc = xs_ref[j - nj]
        for c in range(nj):
            acc = acc + _dot(mix_ref[c], wo_ref[c * tn:(c + 1) * tn, :])
        o_ref[...] = acc


def _attn_tail(x, g, yf, ym, wg, wf, wm, wo, tm, tn):
    r = x.shape[0]
    nj = D_MODEL // tn
    mix_blk = lambda j: jnp.where(j < nj, j, 0)
    mix_col = lambda i, j: (0, mix_blk(j))
    out_col = lambda i, j: (0, jnp.where(j < nj, nj - 1, j - nj))
    return pl.pallas_call(
        functools.partial(_attn_tail_kernel, nj=nj, tn=tn),
        grid=(r // tm, 2 * nj),
        in_specs=[pl.BlockSpec((tm, D_MODEL), lambda i, j: (i, 0)),
                  pl.BlockSpec((1, D_MODEL), lambda i, j: (0, 0)),
                  pl.BlockSpec((tm, FOX_Q_W), lambda i, j: (i, 0)),
                  pl.BlockSpec((tm, MLA_HEADS * MLA_V), lambda i, j: (i, 0)),
                  pl.BlockSpec((D_MODEL, tn), mix_col),
                  pl.BlockSpec((D_MODEL, tn), lambda i, j: (0, nj + mix_blk(j))),
                  pl.BlockSpec((FOX_Q_W, tn), mix_col),
                  pl.BlockSpec((MLA_HEADS * MLA_V, tn), mix_col),
                  pl.BlockSpec((D_MODEL, tn), out_col)],
        out_specs=pl.BlockSpec((tm, tn), lambda i, j: (i, jnp.maximum(j - nj, 0))),
        out_shape=jax.ShapeDtypeStruct((r, D_MODEL), f32),
        scratch_shapes=[pltpu.VMEM((tm, D_MODEL), bf16), pltpu.VMEM((nj, tm, tn), f32),
                        pltpu.VMEM((nj, tm, tn), bf16)],
        compiler_params=_cparams(("arbitrary", "arbitrary")),
        name="attn_tail",
    )(x, g, yf, ym, wg, wg, wf, wm, wo)


def _ffn_kernel(h_ref, g_ref, wg_ref, wu_ref, wd_ref, gf_ref, y_ref, hn_ref, acc_ref):
    j = pl.program_id(1)

    @pl.when(j == 0)
    def _():
        h = h_ref[...]
        hn_ref[...] = _rms(h, g_ref[...]).astype(bf16)
        acc_ref[...] = h

    hn = hn_ref[...]
    a = _dot(hn, wg_ref[...])
    u = _dot(hn, wu_ref[...])
    act = (a / (1.0 + jnp.exp(-a)) * u).astype(bf16)
    acc_ref[...] += _dot(act, wd_ref[...])

    @pl.when(j == pl.num_programs(1) - 1)
    def _():
        y_ref[...] = _rms(acc_ref[...], gf_ref[...])


def _ffn(h, g, wg, wu, wd, gf, tm, tf):
    r = h.shape[0]
    nf = wg.shape[1] // tf
    return pl.pallas_call(
        _ffn_kernel,
        grid=(r // tm, nf),
        in_specs=[pl.BlockSpec((tm, D_MODEL), lambda i, j: (i, 0)),
                  pl.BlockSpec((1, D_MODEL), lambda i, j: (0, 0)),
                  pl.BlockSpec((D_MODEL, tf), lambda i, j: (0, j)),
                  pl.BlockSpec((D_MODEL, tf), lambda i, j: (0, j)),
                  pl.BlockSpec((tf, D_MODEL), lambda i, j: (j, 0)),
                  pl.BlockSpec((1, D_MODEL), lambda i, j: (0, 0))],
        out_specs=pl.BlockSpec((tm, D_MODEL), lambda i, j: (i, 0)),
        out_shape=jax.ShapeDtypeStruct((r, D_MODEL), f32),
        scratch_shapes=[pltpu.VMEM((tm, D_MODEL), bf16), pltpu.VMEM((tm, D_MODEL), f32)],
        compiler_params=_cparams(("arbitrary", "arbitrary")),
        name="ffn",
    )(h, g, wg, wu, wd, gf)


def _rope_tables(pos):
    inv_freq = ROPE_BASE ** (-jnp.arange(0, MLA_ROPE, 2, dtype=f32) / MLA_ROPE)
    ang = pos.astype(f32)[:, None] * inv_freq[None, :]
    c, s = jnp.cos(ang), jnp.sin(ang)
    z = jnp.zeros((pos.shape[0], LANE - MLA_ROPE), f32)
    return jnp.concatenate([c, c, z], axis=1), jnp.concatenate([-s, s, z], axis=1)


def _swap_halves(w):
    half = w.shape[-1] // 2
    return jnp.concatenate([w[..., half:], w[..., :half]], axis=-1)


def kernel(x_prompt, x_sample, cache_fox_k, cache_fox_v, cache_fox_logf, cache_mla_ckv, cache_mla_krope,
           page_table, meta_tokens, attn_norm, w_in, fox_forget_bias, mla_q_norm, mla_kv_norm, mla_w_uq,
           mla_w_uk, mla_w_uv, w_branch_fox, w_branch_mla, w_out, ffn_norm, w_gate, w_up, w_down, final_norm):
    nb, seq, _ = x_prompt.shape
    db = x_sample.shape[0]
    n_pool = cache_fox_k.shape[1]
    past_len = page_table.shape[1] * PAGE_SIZE
    layer = 0

    w0 = w_in[layer]
    o_f = FOX_Q_W + 2 * FOX_KV_W
    o_cq = o_f + FOX_HEADS
    o_ckv = o_cq + MLA_Q_LORA
    o_kr = o_ckv + MLA_KV_LORA
    o_g = o_kr + MLA_ROPE
    w_qkv = w0[:, :o_f].astype(bf16)
    w_kr = w0[:, o_kr:o_g]
    zpad = lambda n: jnp.zeros((D_MODEL, n), f32)
    w_lat = jnp.concatenate([w0[:, o_cq:o_kr], w_kr, zpad(LANE - MLA_ROPE), _swap_halves(w_kr),
                             zpad(LANE - MLA_ROPE), w0[:, o_f:o_cq], zpad(LANE - FOX_HEADS)], axis=1).astype(bf16)
    w_g = w0[:, o_g:].astype(bf16)
    bias_f = jnp.concatenate([fox_forget_bias[layer], jnp.zeros((LANE - FOX_HEADS,), f32)])[None]
    wuq = mla_w_uq[layer].reshape(MLA_Q_LORA, MLA_HEADS, MLA_NOPE + MLA_ROPE)
    zq = jnp.zeros((MLA_Q_LORA, MLA_HEADS, LANE - MLA_ROPE), f32)
    wuq = jnp.concatenate([wuq[..., :MLA_NOPE], wuq[..., MLA_NOPE:], zq,
                           _swap_halves(wuq[..., MLA_NOPE:]), zq], axis=-1)
    wuq = wuq.reshape(MLA_Q_LORA, MLA_HEADS * 3 * LANE).astype(bf16)
    wuk = mla_w_uk[layer].reshape(MLA_KV_LORA, MLA_HEADS * MLA_NOPE).astype(bf16)
    wuv = mla_w_uv[layer].reshape(MLA_KV_LORA, MLA_HEADS * MLA_V).astype(bf16)
    wuk_t = mla_w_uk[layer].transpose(1, 2, 0)
    wuv_h = mla_w_uv[layer].transpose(1, 0, 2).astype(bf16)
    w_bf = w_branch_fox[layer].astype(bf16)
    w_bm = w_branch_mla[layer].astype(bf16)
    w_o = w_out[layer].astype(bf16)
    w_ga = w_gate[layer].astype(bf16)
    w_u = w_up[layer].astype(bf16)
    w_d = w_down[layer].astype(bf16)
    g_attn = attn_norm[layer][None]
    g_ffn = ffn_norm[layer][None]
    g_fin = final_norm[None]
    qn = mla_q_norm[layer][None]
    kvn = mla_kv_norm[layer][None]

    x_main = x_prompt.reshape(nb * seq, D_MODEL)
    x_small = jnp.concatenate([x_sample.reshape(db, D_MODEL), meta_tokens.astype(f32),
                               jnp.zeros((SMALL_ROWS - db - N_META, D_MODEL), f32)], axis=0)
    cos_m, sin_m = _rope_tables(N_META + jnp.arange(seq))
    pos_s = jnp.concatenate([jnp.full((db,), past_len, jnp.int32), jnp.arange(N_META, dtype=jnp.int32),
                             jnp.zeros((SMALL_ROWS - db - N_META,), jnp.int32)])
    cos_s, sin_s = _rope_tables(pos_s)

    def project(x, cos, sin, tm):
        q, k, v, vb = _qkv_proj(x, g_attn, w_qkv, tm)
        cq, ckv, ckvb, kr, krb, lf, lfp = _lat_proj(x, g_attn, w_lat, qn, kvn, bias_f, cos, sin, tm)
        return dict(q=q, k=k, v=v, vb=vb, cq=cq, ckv=ckv, ckvb=ckvb, kr=kr, krb=krb, lf=lf, lfp=lfp)

    pm = project(x_main, cos_m, sin_m, MAIN_TM)
    ps = project(x_small, cos_s, sin_s, SMALL_ROWS)

    meta_rows = slice(db, db + LANE)
    is_meta = (jnp.arange(LANE) < N_META)[:, None]
    lf_meta = jnp.where(is_meta, ps["lfp"][meta_rows], 0.0)[None]
    c_meta = _cumsum(lf_meta, jnp.zeros((1, LANE), f32))
    c_main = _cumsum(pm["lfp"].reshape(nb, seq, LANE), c_meta[0, N_META - 1:N_META])
    c_small = jnp.concatenate([jnp.zeros((db, LANE), f32), c_meta[0]], axis=0)

    qa_m, ka_m = _fox_pack(pm["q"], pm["k"], c_main.reshape(nb * seq, LANE), MAIN_TM)
    _, ka_s = _fox_pack(ps["q"], ps["k"], c_small, SMALL_ROWS)
    zero_meta = lambda a: jnp.where(is_meta, a[meta_rows], jnp.zeros((), a.dtype))
    def key_blocks_t(v):
        return v.reshape(nb * seq // ATTN_TQ, ATTN_TQ, v.shape[1]).transpose(0, 2, 1)

    y_fox = _flash(qa_m, ka_m, key_blocks_t(pm["vb"]), zero_meta(ka_s), zero_meta(ps["vb"]).T,
                   nb, seq, FOX_HEADS, ATTN_HEADS_PER_STEP, FOX_GROUP, ATTN_TQ)
    qm_m = _mla_q(pm["cq"], wuq, cos_m, sin_m, MAIN_TM, bf16)
    km_m, vm_m = _mla_kv(pm["ckvb"], pm["krb"], wuk, wuv, MAIN_TM)
    km_s, vm_s = _mla_kv(ps["ckvb"], ps["krb"], wuk, wuv, SMALL_ROWS)
    y_mla = _flash(qm_m, km_m, key_blocks_t(vm_m), zero_meta(km_s), zero_meta(vm_s).T,
                   nb, seq, MLA_HEADS, ATTN_HEADS_PER_STEP, 1, ATTN_TQ)

    qm_s = _mla_q(ps["cq"], wuq, cos_s, sin_s, SMALL_ROWS, f32)
    q_lat = _mla_qlat(qm_s, wuk_t)[:db].reshape(db, MLA_HEADS, MLA_KV_LORA)
    q_rope = qm_s[:db].reshape(db, MLA_HEADS, AUG)[:, :, LANE:LANE + MLA_ROPE]
    qf = ps["q"][:db].astype(f32).reshape(db, FOX_HEADS, FOX_HEAD_DIM)
    cn = jnp.broadcast_to(ps["lf"][:db, :, None], (db, FOX_HEADS, LANE))
    kn = jnp.repeat(ps["k"][:db].reshape(db, FOX_KV_HEADS, FOX_HEAD_DIM), FOX_GROUP, axis=1)
    vn = jnp.repeat(ps["v"][:db].reshape(db, FOX_KV_HEADS, FOX_HEAD_DIM), FOX_GROUP, axis=1)
    o_fox, o_lat = _decode(
        page_table, qf, cn, kn, vn, q_lat, q_rope,
        ps["ckv"][:db].reshape(db, 1, MLA_KV_LORA), ps["kr"][:db].reshape(db, 1, MLA_ROPE),
        cache_fox_k[layer].reshape(n_pool * PAGE_SIZE * FOX_KV_HEADS, FOX_HEAD_DIM),
        cache_fox_v[layer].reshape(n_pool * PAGE_SIZE * FOX_KV_HEADS, FOX_HEAD_DIM),
        jnp.swapaxes(cache_fox_logf[layer], 1, 2), cache_mla_ckv[layer],
        jnp.swapaxes(cache_mla_krope[layer], 1, 2))
    pad_rows = lambda a: jnp.concatenate([a, jnp.zeros((SMALL_ROWS - db,) + a.shape[1:], a.dtype)], axis=0)
    y_fox_s = pad_rows(o_fox.reshape(db, FOX_Q_W).astype(bf16))
    y_mla_s = _mla_uv(pad_rows(o_lat).transpose(1, 0, 2).astype(bf16), wuv_h)

    def tail(x, yf, ym, tm):
        h1 = _attn_tail(x, g_attn, yf, ym, w_g, w_bf, w_bm, w_o, tm, DENSE_TN)
        return _ffn(h1, g_ffn, w_ga, w_u, w_d, g_fin, tm, DENSE_TN)

    y_main = tail(x_main, y_fox, y_mla, MAIN_TM)
    y_small = tail(x_small, y_fox_s, y_mla_s, SMALL_ROWS)

    def prompt_out(name, tail_shape):
        meta = jnp.broadcast_to(ps[name][db:db + N_META][None], (nb, N_META) + ps[name].shape[1:])
        full = jnp.concatenate([meta, pm[name].reshape((nb, seq) + pm[name].shape[1:])], axis=1)
        return full.reshape((1, nb, seq + N_META) + tail_shape)

    def sample_out(name, tail_shape):
        return ps[name][:db].reshape((1, db, 1) + tail_shape)

    kv_shape = (FOX_KV_HEADS, FOX_HEAD_DIM)
    return (y_main.reshape(nb, seq, D_MODEL), y_small[:db].reshape(db, 1, D_MODEL),
            prompt_out("k", kv_shape), prompt_out("v", kv_shape), prompt_out("lf", (FOX_HEADS,)),
            prompt_out("ckv", (MLA_KV_LORA,)), prompt_out("kr", (MLA_ROPE,)),
            sample_out("k", kv_shape), sample_out("v", kv_shape), sample_out("lf", (FOX_HEADS,)),
            sample_out("ckv", (MLA_KV_LORA,)), sample_out("kr", (MLA_ROPE,)))
```
